```python
import jax, jax.numpy as jnp
from jax import lax
import numpy as np

D_MODEL = 1024
BATCH = 4
SEQ = 8192
DEPTH = 1

PLE_DIM = 256
SB_HEADS = 8
SB_HEAD_DIM = 64
SB_WIDTH = SB_HEADS * SB_HEAD_DIM
SB_BLOCK = 128
ML_HEADS = 4
ML_HEAD_DIM = 128
ML_WIDTH = ML_HEADS * ML_HEAD_DIM
ML_CHUNK = 128
CONV_WIDTH = 4
PEER_HEADS = 8
PEER_KEYS = 128
PEER_EXPERTS = PEER_KEYS * PEER_KEYS
PEER_QDIM = 256
PEER_HALF = PEER_QDIM // 2
PEER_TOPK = 16
PEER_BLOCK = 128
ALPHA = (2.0 * DEPTH) ** 0.25
BETA = (8.0 * DEPTH) ** -0.25
LN_EPS = 1e-5
IN_SIZES = (SB_WIDTH, SB_WIDTH, SB_WIDTH, ML_WIDTH, ML_WIDTH, ML_WIDTH, ML_WIDTH, ML_HEADS, ML_HEADS, D_MODEL, D_MODEL)
IN_WIDTH = sum(IN_SIZES)

kernel_name = 'hybrid_stickbreak_mlstm_peer_deepnorm'


def layer_norm(x, g, b):
    xf = x.astype(jnp.float32)
    mu = jnp.mean(xf, axis=-1, keepdims=True)
    var = jnp.mean(jnp.square(xf - mu), axis=-1, keepdims=True)
    return ((xf - mu) * lax.rsqrt(var + LN_EPS) * g + b).astype(x.dtype)


def split_heads(t, n_heads):
    b, s, _ = t.shape
    return t.reshape(b, s, n_heads, -1).transpose(0, 2, 1, 3)


def merge_heads(t):
    b, h, s, d = t.shape
    return t.transpose(0, 2, 1, 3).reshape(b, s, h * d)


def causal_conv(x, w, b):
    k_w = w.shape[0]
    s = x.shape[1]
    xp = jnp.pad(x, ((0, 0), (k_w - 1, 0), (0, 0)))
    y = b
    for j in range(k_w):
        y = y + w[j] * xp[:, k_w - 1 - j:k_w - 1 - j + s]
    return y


def stick_breaking_attention(q, k, v):
    s_len, dh = q.shape[2], q.shape[3]
    scale = dh ** -0.5
    outs = []
    for blk in range(s_len // SB_BLOCK):
        t0 = blk * SB_BLOCK
        kl = t0 + SB_BLOCK
        z = jnp.einsum('bhtd,bhsd->bhts', q[:, :, t0:kl], k[:, :, :kl]).astype(jnp.float32) * scale
        t_idx = t0 + jnp.arange(SB_BLOCK)[:, None]
        s_idx = jnp.arange(kl)[None, :]
        causal = s_idx < t_idx
        log_beta = jax.nn.log_sigmoid(z)
        log_one_minus = jnp.where(causal, jax.nn.log_sigmoid(-z), 0.0)
        tail = lax.cumsum(log_one_minus, axis=3, reverse=True) - log_one_minus
        w = jnp.where(causal, jnp.exp(log_beta + tail), 0.0)
        outs.append(jnp.einsum('bhts,bhsd->bhtd', w.astype(v.dtype), v[:, :, :kl]))
    return jnp.concatenate(outs, axis=2)


def mlstm_chunkwise(q, k, v, log_i, log_f):
    b_, h_, s_len, d = q.shape
    L = ML_CHUNK
    nc = s_len // L
    f32 = jnp.float32
    q = q.astype(f32).reshape(b_, h_, nc, L, d)
    k = (k.astype(f32) * d ** -0.5).reshape(b_, h_, nc, L, d)
    v = v.astype(f32).reshape(b_, h_, nc, L, d)
    li = log_i.reshape(b_, h_, nc, L)
    bcum = jnp.cumsum(log_f.reshape(b_, h_, nc, L), axis=-1)
    b_last = bcum[..., -1]
    w_end = b_last[..., None] - bcum + li
    m_loc = jnp.max(w_end, axis=-1)
    e_end = jnp.exp(w_end - m_loc[..., None])
    c_loc = jnp.einsum('bhcsd,bhcse->bhcde', e_end[..., None] * v, k)
    n_loc = jnp.einsum('bhcs,bhcse->bhce', e_end, k)

    def step(carry, inp):
        c_st, n_st, m_st = carry
        cl, nl, ml, bl = inp
        m_new = jnp.maximum(bl + m_st, ml)
        a = jnp.exp(bl + m_st - m_new)
        g = jnp.exp(ml - m_new)
        c_new = a[..., None, None] * c_st + g[..., None, None] * cl
        n_new = a[..., None] * n_st + g[..., None] * nl
        return (c_new, n_new, m_new), (c_st, n_st, m_st)

    init = (jnp.zeros((b_, h_, d, d), f32), jnp.zeros((b_, h_, d), f32), jnp.zeros((b_, h_), f32))
    xs = (jnp.moveaxis(c_loc, 2, 0), jnp.moveaxis(n_loc, 2, 0), jnp.moveaxis(m_loc, 2, 0), jnp.moveaxis(b_last, 2, 0))
    _, (c_prev, n_prev, m_prev) = lax.scan(step, init, xs)
    c_prev = jnp.moveaxis(c_prev, 0, 2)
    n_prev = jnp.moveaxis(n_prev, 0, 2)
    m_prev = jnp.moveaxis(m_prev, 0, 2)
    d_log = bcum[..., :, None] - bcum[..., None, :] + li[..., None, :]
    tri = jnp.tril(jnp.ones((L, L), dtype=bool))
    d_log = jnp.where(tri, d_log, -jnp.inf)
    inter_log = bcum + m_prev[..., None]
    m_t = jnp.maximum(inter_log, jnp.max(d_log, axis=-1))
    a_t = jnp.exp(inter_log - m_t)
    s_w = jnp.einsum('bhctd,bhcsd->bhcts', q, k) * jnp.exp(d_log - m_t[..., None])
    num = a_t[..., None] * jnp.einsum('bhcde,bhcte->bhctd', c_prev, q) + jnp.einsum('bhcts,bhcsd->bhctd', s_w, v)
    den = a_t * jnp.einsum('bhce,bhcte->bhct', n_prev, q) + jnp.sum(s_w, axis=-1)
    h = num / jnp.maximum(jnp.abs(den), jnp.exp(-m_t))[..., None]
    return h.reshape(b_, h_, s_len, d)


def token_mixer(h, w_in, b_igate, b_fgate, conv_w, conv_b, w_branch_sb, w_branch_ml, w_out):
    proj = h @ w_in
    offs = []
    acc = 0
    for sz in IN_SIZES[:-1]:
        acc += sz
        offs.append(acc)
    sb_q, sb_k, sb_v, ml_q, ml_k, ml_v, ml_o, ml_i, ml_f, gate_sb, gate_ml = jnp.split(proj, offs, axis=-1)
    o_sb = merge_heads(stick_breaking_attention(split_heads(sb_q, SB_HEADS), split_heads(sb_k, SB_HEADS), split_heads(sb_v, SB_HEADS)))
    qk = jax.nn.silu(causal_conv(jnp.concatenate([ml_q, ml_k], axis=-1), conv_w, conv_b))
    ml_q, ml_k = jnp.split(qk, 2, axis=-1)
    log_i = (ml_i + b_igate).astype(jnp.float32).transpose(0, 2, 1)
    log_f = jax.nn.log_sigmoid((ml_f + b_fgate).astype(jnp.float32)).transpose(0, 2, 1)
    h_ml = mlstm_chunkwise(split_heads(ml_q, ML_HEADS), split_heads(ml_k, ML_HEADS), split_heads(ml_v, ML_HEADS), log_i, log_f)
    o_ml = jax.nn.sigmoid(ml_o) * merge_heads(h_ml).astype(h.dtype)
    y = jax.nn.sigmoid(gate_sb) * (o_sb @ w_branch_sb) + jax.nn.sigmoid(gate_ml) * (o_ml @ w_branch_ml)
    return y @ w_out


def peer_ffn(h, wq, k1, k2, u_tab, v_tab):
    b_, s_len, d = h.shape
    q = (h @ wq).astype(jnp.float32).reshape(b_, s_len, PEER_HEADS, 2, PEER_HALF)
    s1 = jnp.einsum('bshd,nd->bshn', q[..., 0, :], k1.astype(jnp.float32))
    s2 = jnp.einsum('bshd,nd->bshn', q[..., 1, :], k2.astype(jnp.float32))
    v1, i1 = lax.top_k(s1, PEER_TOPK)
    v2, i2 = lax.top_k(s2, PEER_TOPK)
    cand = (v1[..., :, None] + v2[..., None, :]).reshape(b_, s_len, PEER_HEADS, PEER_TOPK * PEER_TOPK)
    cidx = (i1[..., :, None] * PEER_KEYS + i2[..., None, :]).reshape(b_, s_len, PEER_HEADS, PEER_TOPK * PEER_TOPK)
    top_s, pos = lax.top_k(cand, PEER_TOPK)
    idx = jnp.take_along_axis(cidx, pos, axis=-1)
    gates = jax.nn.softmax(top_s, axis=-1).astype(h.dtype)
    nb = s_len // PEER_BLOCK

    def blockify(t):
        return jnp.moveaxis(t.reshape((b_, nb, PEER_BLOCK) + t.shape[2:]), 1, 0)

    def expert_block(args):
        hb, ib, gb = args
        ue = jnp.take(u_tab, ib, axis=0)
        act = jax.nn.gelu(jnp.einsum('btd,bthkd->bthk', hb, ue), approximate=False)
        ve = jnp.take(v_tab, ib, axis=0)
        return jnp.einsum('bthk,bthkd->btd', gb * act, ve)

    out = lax.map(expert_block, (blockify(h), blockify(idx), blockify(gates)))
    return jnp.moveaxis(out, 0, 1).reshape(b_, s_len, d)


def setup_inputs(seed: int = 0) -> dict:
    key = jax.random.key(seed)
    ks = jax.random.split(key, 24)
    f32 = jnp.float32
    nrm = lambda k, shape, s: jax.random.normal(k, shape, f32) * s
    dn = DEPTH
    b_f = jnp.broadcast_to(jnp.linspace(3.0, 6.0, ML_HEADS, dtype=f32), (dn, ML_HEADS)) + nrm(ks[5], (dn, ML_HEADS), 0.01)
    return {
        'x': nrm(ks[0], (BATCH, SEQ, D_MODEL), 1.0),
        'p': nrm(ks[1], (DEPTH, BATCH, SEQ, PLE_DIM), 1.0),
        'ln0_g': 1.0 + nrm(ks[2], (D_MODEL,), 0.01),
        'ln0_b': nrm(ks[3], (D_MODEL,), 0.01),
        'w_in': nrm(ks[4], (dn, D_MODEL, IN_WIDTH), D_MODEL ** -0.5),
        'b_igate': nrm(ks[6], (dn, ML_HEADS), 0.1),
        'b_fgate': b_f,
        'conv_w': nrm(ks[7], (dn, CONV_WIDTH, 2 * ML_WIDTH), CONV_WIDTH ** -0.5),
        'conv_b': nrm(ks[8], (dn, 2 * ML_WIDTH), 0.01),
        'w_branch_sb': nrm(ks[9], (dn, SB_WIDTH, D_MODEL), BETA * SB_WIDTH ** -0.5),
        'w_branch_ml': nrm(ks[10], (dn, ML_WIDTH, D_MODEL), BETA * ML_WIDTH ** -0.5),
        'w_out': nrm(ks[11], (dn, D_MODEL, D_MODEL), BETA * D_MODEL ** -0.5),
        'ln1_g': 1.0 + nrm(ks[12], (dn, D_MODEL), 0.01),
        'ln1_b': nrm(ks[13], (dn, D_MODEL), 0.01),
        'peer_wq': nrm(ks[14], (dn, D_MODEL, PEER_HEADS * PEER_QDIM), D_MODEL ** -0.5),
        'peer_k1': nrm(ks[15], (dn, PEER_KEYS, PEER_HALF), PEER_HALF ** -0.5),
        'peer_k2': nrm(ks[16], (dn, PEER_KEYS, PEER_HALF), PEER_HALF ** -0.5),
        'peer_u': nrm(ks[17], (dn, PEER_EXPERTS, D_MODEL), D_MODEL ** -0.5),
        'peer_v': nrm(ks[18], (dn, PEER_EXPERTS, D_MODEL), BETA * PEER_HEADS ** -0.5),
        'w_ple_gate': nrm(ks[19], (dn, D_MODEL, D_MODEL), D_MODEL ** -0.5),
        'w_ple': nrm(ks[20], (dn, PLE_DIM, D_MODEL), PLE_DIM ** -0.5),
        'ln2_g': 1.0 + nrm(ks[21], (dn, D_MODEL), 0.01),
        'ln2_b': nrm(ks[22], (dn, D_MODEL), 0.01),
    }


def reference(x, p, ln0_g, ln0_b, w_in, b_igate, b_fgate, conv_w, conv_b, w_branch_sb, w_branch_ml, w_out, ln1_g, ln1_b, peer_wq, peer_k1, peer_k2, peer_u, peer_v, w_ple_gate, w_ple, ln2_g, ln2_b):
    h = layer_norm(x, ln0_g, ln0_b)
    for i in range(DEPTH):
        mix = token_mixer(h, w_in[i], b_igate[i], b_fgate[i], conv_w[i], conv_b[i], w_branch_sb[i], w_branch_ml[i], w_out[i])
        h = layer_norm(ALPHA * h + mix, ln1_g[i], ln1_b[i])
        ffn = peer_ffn(h, peer_wq[i], peer_k1[i], peer_k2[i], peer_u[i], peer_v[i])
        ple = jax.nn.sigmoid(h @ w_ple_gate[i]) * (p[i] @ w_ple[i])
        h = layer_norm(ALPHA * h + ffn + ple, ln2_g[i], ln2_b[i])
    return h
```

```python
import functools

import jax
import jax.numpy as jnp
from jax import lax
from jax.experimental import pallas as pl
from jax.experimental.pallas import tpu as pltpu

F32 = jnp.float32
BF16 = jnp.bfloat16
I32 = jnp.int32

D_MODEL = 1024
SB_HEADS = 8
SB_HEAD_DIM = 64
SB_WIDTH = SB_HEADS * SB_HEAD_DIM
ML_HEADS = 4
ML_HEAD_DIM = 128
ML_WIDTH = ML_HEADS * ML_HEAD_DIM
ML_CHUNK = 128
CONV_WIDTH = 4
PEER_HEADS = 8
PEER_KEYS = 128
PEER_QDIM = 256
PEER_HALF = PEER_QDIM // 2
PEER_TOPK = 16
PEER_BLOCK = 128
DEPTH = 1
ALPHA = (2.0 * DEPTH) ** 0.25
LN_EPS = 1e-5

LANES = 128
VMEM_LIMIT = 56 * 1024 * 1024
SB_SKIP_LOG = -104.0

PA_WIDTH = 3 * SB_WIDTH + 4 * ML_WIDTH


def _cparams(*sem):
    return pltpu.CompilerParams(dimension_semantics=sem, vmem_limit_bytes=VMEM_LIMIT)


def _layer_norm(x, g, b):
    mu = jnp.mean(x, axis=-1, keepdims=True)
    xc = x - mu
    var = jnp.mean(xc * xc, axis=-1, keepdims=True)
    return xc * lax.rsqrt(var + LN_EPS) * g + b


def _log_sigmoid(z):
    return jnp.minimum(z, 0.0) - jnp.log1p(jnp.exp(-jnp.abs(z)))


def _sigmoid(z):
    return 1.0 / (1.0 + jnp.exp(-z))


def _dot(a, b):
    return jnp.dot(a, b, preferred_element_type=F32)


def _dot_nt(a, b):
    return lax.dot_general(a, b, (((1,), (1,)), ((), ())), preferred_element_type=F32)


def _dot_tn(a, b):
    return lax.dot_general(a, b, (((0,), (0,)), ((), ())), preferred_element_type=F32)


def _inproj_kernel(x_ref, g_ref, b_ref, wa_ref, wif_ref, wg_ref, oa_ref, oif_ref, og_ref, *, cw):
    h = _layer_norm(x_ref[...], g_ref[...], b_ref[...]).astype(BF16)
    for j in range(0, wa_ref.shape[1], cw):
        oa_ref[:, j:j + cw] = _dot(h, wa_ref[:, j:j + cw]).astype(BF16)
    oif_ref[...] = _dot(h, wif_ref[...])
    for j in range(0, wg_ref.shape[1], cw):
        og_ref[:, j:j + cw] = _dot(h, wg_ref[:, j:j + cw]).astype(BF16)


def _inproj(x2, g, b, wa, wif, wg, tm=512):
    n, d = x2.shape
    const = lambda i: (0, 0)
    return pl.pallas_call(
        functools.partial(_inproj_kernel, cw=512),
        grid=(n // tm,),
        in_specs=[
            pl.BlockSpec((tm, d), lambda i: (i, 0)),
            pl.BlockSpec((1, d), const),
            pl.BlockSpec((1, d), const),
            pl.BlockSpec(wa.shape, const),
            pl.BlockSpec(wif.shape, const),
            pl.BlockSpec(wg.shape, const),
        ],
        out_specs=[
            pl.BlockSpec((tm, wa.shape[1]), lambda i: (i, 0)),
            pl.BlockSpec((tm, wif.shape[1]), lambda i: (i, 0)),
            pl.BlockSpec((tm, wg.shape[1]), lambda i: (i, 0)),
        ],
        out_shape=[
            jax.ShapeDtypeStruct((n, wa.shape[1]), BF16),
            jax.ShapeDtypeStruct((n, wif.shape[1]), F32),
            jax.ShapeDtypeStruct((n, wg.shape[1]), BF16),
        ],
        compiler_params=_cparams("parallel"),
        name="ln_inproj",
    )(x2, g, b, wa, wif, wg)


def _sb_kernel(q_ref, k_ref, v_ref, o_ref, *, tq):
    qi = pl.program_id(2)
    q = q_ref[0]
    lane = lax.broadcasted_iota(I32, (1, LANES), 1)
    row = lax.broadcasted_iota(I32, (tq, tq), 0)
    col = lax.broadcasted_iota(I32, (tq, tq), 1)
    causal = col < row
    later = (row > col).astype(BF16)
    scale = SB_HEAD_DIM ** -0.5

    def tile(qh, kb, carry, masked):
        off = pl.multiple_of(kb * tq, tq)
        k_blk = k_ref[0, pl.ds(off, tq), :]
        v_blk = v_ref[0, pl.ds(off, tq), :]
        z = _dot_nt(qh, k_blk)
        lb = _log_sigmoid(z)
        lom = lb - z
        if masked:
            lom = jnp.where(causal, lom, 0.0)
        hi = lom.astype(BF16)
        lo = (lom - hi.astype(F32)).astype(BF16)
        tail = _dot(hi, later) + _dot(lo, later) + carry
        w = jnp.exp(lb + tail)
        if masked:
            w = jnp.where(causal, w, 0.0)
        contrib = _dot(w.astype(BF16), v_blk)
        return contrib, carry + jnp.sum(lom, axis=1, keepdims=True)

    acc_total = jnp.zeros((tq, LANES), F32)
    for hh in range(2):
        hmask = (lane >= SB_HEAD_DIM * hh) & (lane < SB_HEAD_DIM * (hh + 1))
        qh = (jnp.where(hmask, q, jnp.zeros_like(q)).astype(F32) * scale).astype(BF16)
        acc, carry = tile(qh, qi, jnp.zeros((tq, 1), F32), True)

        def cond(st):
            kb, _, _, cmax = st
            return jnp.logical_and(kb >= 0, cmax > SB_SKIP_LOG)

        def body(st):
            kb, acc, carry, _ = st
            contrib, carry = tile(qh, kb, carry, False)
            return kb - 1, acc + contrib, carry, jnp.max(carry)

        _, acc, _, _ = lax.while_loop(cond, body, (qi - 1, acc, carry, jnp.max(carry)))
        acc_total = jnp.where(hmask, acc, acc_total)
    o_ref[0] = acc_total.astype(o_ref.dtype)


def _sb_attention(pa3, tq=128):
    b, s, _ = pa3.shape
    npair = SB_WIDTH // LANES
    return pl.pallas_call(
        functools.partial(_sb_kernel, tq=tq),
        grid=(b, npair, s // tq),
        in_specs=[
            pl.BlockSpec((1, tq, LANES), lambda bi, hp, qi: (bi, qi, hp)),
            pl.BlockSpec((1, s, LANES), lambda bi, hp, qi: (bi, 0, npair + hp)),
            pl.BlockSpec((1, s, LANES), lambda bi, hp, qi: (bi, 0, 2 * npair + hp)),
        ],
        out_specs=pl.BlockSpec((1, tq, LANES), lambda bi, hp, qi: (bi, qi, hp)),
        out_shape=jax.ShapeDtypeStruct((b, s, SB_WIDTH), BF16),
        compiler_params=_cparams("parallel", "parallel", "arbitrary"),
        name="sb_attention",
    )(pa3, pa3, pa3)


CONV_HALO = 16


def _conv_kernel(x_ref, prev_ref, w_ref, b_ref, o_ref, buf_ref, *, ts):
    si = pl.program_id(1)
    is_k = pl.program_id(2)
    prev = prev_ref[0].astype(F32)
    buf_ref[0:CONV_HALO, :] = jnp.where(si == 0, 0.0, prev)
    buf_ref[CONV_HALO:, :] = x_ref[0].astype(F32)
    y = b_ref[...] + w_ref[0:1, :] * buf_ref[CONV_HALO:, :]
    for j in range(1, CONV_WIDTH):
        y = y + w_ref[j:j + 1, :] * buf_ref[CONV_HALO - j:CONV_HALO - j + ts, :]
    y = y * _sigmoid(y)
    y = y * jnp.where(is_k == 1, ML_HEAD_DIM ** -0.5, 1.0)
    o_ref[0] = y.astype(o_ref.dtype)


def _conv_silu(pa3, conv_w, conv_b, ts=1024):
    b, s, _ = pa3.shape
    ts = min(ts, s)
    cb = ML_WIDTH
    base = 3 * SB_WIDTH // cb
    hb = ts // CONV_HALO
    return pl.pallas_call(
        functools.partial(_conv_kernel, ts=ts),
        grid=(b, s // ts, 2),
        in_specs=[
            pl.BlockSpec((1, ts, cb), lambda bi, si, j: (bi, si, base + j)),
            pl.BlockSpec((1, CONV_HALO, cb), lambda bi, si, j: (bi, jnp.maximum(si * hb - 1, 0), base + j)),
            pl.BlockSpec((CONV_WIDTH, cb), lambda bi, si, j: (0, j)),
            pl.BlockSpec((1, cb), lambda bi, si, j: (0, j)),
        ],
        out_specs=pl.BlockSpec((1, ts, cb), lambda bi, si, j: (bi, si, j)),
        out_shape=jax.ShapeDtypeStruct((b, s, 2 * ML_WIDTH), BF16),
        scratch_shapes=[pltpu.VMEM((ts + CONV_HALO, cb), F32)],
        compiler_params=_cparams("parallel", "parallel", "parallel"),
        name="conv_silu",
    )(pa3, pa3, conv_w, conv_b)


def _mlstm_kernel(q_ref, k_ref, v_ref, g_ref, gb_ref, o_ref):
    L = ML_CHUNK
    head = pl.program_id(1)
    nc = q_ref.shape[1] // L
    r = lax.broadcasted_iota(I32, (L, L), 0)
    c = lax.broadcasted_iota(I32, (L, L), 1)
    eye = r == c
    sub = lax.broadcasted_iota(I32, (2 * ML_HEADS, L), 0)

    def to_col(row):
        return jnp.sum(jnp.where(eye, jnp.broadcast_to(row, (L, L)), 0.0), axis=1, keepdims=True)

    def chunk(ci, st):
        c_st, n_st, m_st = st
        off = pl.multiple_of(ci * L, L)
        q = q_ref[0, pl.ds(off, L), :]
        k = k_ref[0, pl.ds(off, L), :]
        v = v_ref[0, pl.ds(off, L), :]
        g = g_ref[0, :, pl.ds(off, L)] + gb_ref[...]
        li_row = jnp.sum(jnp.where(sub == head, g, 0.0), axis=0, keepdims=True)
        lf_row = _log_sigmoid(jnp.sum(jnp.where(sub == head + ML_HEADS, g, 0.0), axis=0, keepdims=True))
        lf_b = jnp.broadcast_to(lf_row, (L, L))
        bcum_col = jnp.sum(jnp.where(c <= r, lf_b, 0.0), axis=1, keepdims=True)
        lf_col = to_col(lf_row)
        li_col = to_col(li_row)
        bcum_row = jnp.sum(jnp.where(r <= c, jnp.broadcast_to(lf_col, (L, L)), 0.0), axis=0, keepdims=True)
        b_last = jnp.sum(lf_row, axis=1, keepdims=True)

        d_log = jnp.where(c <= r, bcum_col - bcum_row + li_row, -jnp.inf)
        inter = bcum_col + m_st
        m_t = jnp.maximum(inter, jnp.max(d_log, axis=1, keepdims=True))
        a_t = jnp.exp(inter - m_t)
        s_w = _dot_nt(q, k) * jnp.exp(d_log - m_t)
        num = a_t * _dot_nt(q, c_st.astype(BF16)) + _dot(s_w.astype(BF16), v)
        den = a_t * jnp.sum(q.astype(F32) * n_st, axis=1, keepdims=True) + jnp.sum(s_w, axis=1, keepdims=True)
        o_ref[0, pl.ds(off, L), :] = (num / jnp.maximum(jnp.abs(den), jnp.exp(-m_t))).astype(o_ref.dtype)

        w_end = b_last - bcum_col + li_col
        m_loc = jnp.max(w_end, axis=0, keepdims=True)
        e_end = jnp.exp(w_end - m_loc)
        c_loc = _dot_tn((e_end * v.astype(F32)).astype(BF16), k)
        n_loc = jnp.sum(e_end * k.astype(F32), axis=0, keepdims=True)
        m_new = jnp.maximum(b_last + m_st, m_loc)
        a = jnp.exp(b_last + m_st - m_new)
        gg = jnp.exp(m_loc - m_new)
        return a * c_st + gg * c_loc, a * n_st + gg * n_loc, m_new

    init = (jnp.zeros((ML_HEAD_DIM, ML_HEAD_DIM), F32), jnp.zeros((1, ML_HEAD_DIM), F32), jnp.zeros((1, 1), F32))
    lax.fori_loop(0, nc, chunk, init)


def _mlstm(qk3, pa3, grow, gbias):
    b, s, _ = pa3.shape
    vbase = (3 * SB_WIDTH + 2 * ML_WIDTH) // ML_HEAD_DIM
    return pl.pallas_call(
        _mlstm_kernel,
        grid=(b, ML_HEADS),
        in_specs=[
            pl.BlockSpec((1, s, ML_HEAD_DIM), lambda bi, h: (bi, 0, h)),
            pl.BlockSpec((1, s, ML_HEAD_DIM), lambda bi, h: (bi, 0, ML_HEADS + h)),
            pl.BlockSpec((1, s, ML_HEAD_DIM), lambda bi, h: (bi, 0, vbase + h)),
            pl.BlockSpec((1, 2 * ML_HEADS, s), lambda bi, h: (bi, 0, 0)),
            pl.BlockSpec((2 * ML_HEADS, 1), lambda bi, h: (0, 0)),
        ],
        out_specs=pl.BlockSpec((1, s, ML_HEAD_DIM), lambda bi, h: (bi, 0, h)),
        out_shape=jax.ShapeDtypeStruct((b, s, ML_WIDTH), BF16),
        compiler_params=_cparams("parallel", "parallel"),
        name="mlstm",
    )(qk3, qk3, pa3, grow, gbias)


def _merge_kernel(x_ref, osb_ref, hml_ref, mlo_ref, gate_ref, g0_ref, b0_ref, wsb_ref, wml_ref, wout_ref,
                  g1_ref, b1_ref, o_ref):
    h0 = _layer_norm(x_ref[...], g0_ref[...], b0_ref[...])
    o_ml = (_sigmoid(mlo_ref[...].astype(F32)) * hml_ref[...].astype(F32)).astype(BF16)
    y = _sigmoid(gate_ref[:, :D_MODEL].astype(F32)) * _dot(osb_ref[...], wsb_ref[...])
    y = y + _sigmoid(gate_ref[:, D_MODEL:].astype(F32)) * _dot(o_ml, wml_ref[...])
    mix = _dot(y.astype(BF16), wout_ref[...])
    o_ref[...] = _layer_norm(ALPHA * h0 + mix, g1_ref[...], b1_ref[...])


def _merge(x2, osb, hml, pa, gate, g0, b0, wsb, wml, wout, g1, b1, tm=512):
    n, d = x2.shape
    const = lambda i: (0, 0)
    rowblk = lambda w: pl.BlockSpec((tm, w), lambda i: (i, 0))
    vec = pl.BlockSpec((1, d), const)
    mlo_blk = (3 * SB_WIDTH + 3 * ML_WIDTH) // ML_WIDTH
    return pl.pallas_call(
        _merge_kernel,
        grid=(n // tm,),
        in_specs=[
            rowblk(d), rowblk(SB_WIDTH), rowblk(ML_WIDTH),
            pl.BlockSpec((tm, ML_WIDTH), lambda i: (i, mlo_blk)),
            rowblk(2 * d), vec, vec,
            pl.BlockSpec(wsb.shape, const), pl.BlockSpec(wml.shape, const), pl.BlockSpec(wout.shape, const),
            vec, vec,
        ],
        out_specs=rowblk(d),
        out_shape=jax.ShapeDtypeStruct((n, d), F32),
        compiler_params=_cparams("parallel"),
        name="merge_outproj_ln",
    )(x2, osb, hml, pa, gate, g0, b0, wsb, wml, wout, g1, b1)


def _topk_rows(s, k):
    n = s.shape[0]
    iota = lax.broadcasted_iota(I32, s.shape, 0)
    vals, idxs = [], []
    for _ in range(k):
        m = jnp.max(s, axis=0, keepdims=True)
        am = jnp.min(jnp.where(s == m, iota, n), axis=0, keepdims=True)
        vals.append(m)
        idxs.append(am)
        s = jnp.where(iota == am, -jnp.inf, s)
    return jnp.concatenate(vals, axis=0), jnp.concatenate(idxs, axis=0)


def _select_rows(sel, table):
    out = jnp.zeros(sel.shape, table.dtype)
    for r_ in range(table.shape[0]):
        out = jnp.where(sel == r_, table[r_:r_ + 1, :], out)
    return out


def _peer_route_kernel(h_ref, wq_ref, k1_ref, k2_ref, idx_ref, gate_ref, q_scr, *, tt):
    q_scr[...] = _dot(h_ref[...].astype(BF16), wq_ref[...]).astype(BF16)
    nsub = h_ref.shape[0] // tt

    def one(it, _):
        head = it % PEER_HEADS
        sub = it // PEER_HEADS
        roff = pl.multiple_of(sub * tt, tt)
        coff = pl.multiple_of(head * PEER_QDIM, PEER_QDIM)
        q1 = q_scr[pl.ds(roff, tt), pl.ds(coff, PEER_HALF)]
        q2 = q_scr[pl.ds(roff, tt), pl.ds(coff + PEER_HALF, PEER_HALF)]
        v1, i1 = _topk_rows(_dot_nt(k1_ref[...], q1), PEER_TOPK)
        v2, i2 = _topk_rows(_dot_nt(k2_ref[...], q2), PEER_TOPK)
        cand = jnp.concatenate([v1[i:i + 1, :] + v2 for i in range(PEER_TOPK)], axis=0)
        top_s, pos = _topk_rows(cand, PEER_TOPK)
        e1 = _select_rows(pos // PEER_TOPK, i1)
        e2 = _select_rows(pos % PEER_TOPK, i2)
        ex = jnp.exp(top_s - top_s[0:1, :])
        gates = ex / jnp.sum(ex, axis=0, keepdims=True)
        hoff = pl.multiple_of(head * PEER_TOPK, PEER_TOPK)
        idx_ref[pl.ds(hoff, PEER_TOPK), pl.ds(roff, tt)] = e1 * PEER_KEYS + e2
        gate_ref[pl.ds(hoff, PEER_TOPK), pl.ds(roff, tt)] = gates
        return 0

    lax.fori_loop(0, nsub * PEER_HEADS, one, 0)


def _peer_route(h1, wq, k1, k2, tm=256, tt=128):
    n, d = h1.shape
    const = lambda i: (0, 0)
    nsel = PEER_HEADS * PEER_TOPK
    return pl.pallas_call(
        functools.partial(_peer_route_kernel, tt=tt),
        grid=(n // tm,),
        in_specs=[
            pl.BlockSpec((tm, d), lambda i: (i, 0)),
            pl.BlockSpec(wq.shape, const),
            pl.BlockSpec(k1.shape, const),
            pl.BlockSpec(k2.shape, const),
        ],
        out_specs=[pl.BlockSpec((nsel, tm), lambda i: (0, i)), pl.BlockSpec((nsel, tm), lambda i: (0, i))],
        out_shape=[jax.ShapeDtypeStruct((nsel, n), I32), jax.ShapeDtypeStruct((nsel, n), F32)],
        scratch_shapes=[pltpu.VMEM((tm, PEER_HEADS * PEER_QDIM), BF16)],
        compiler_params=_cparams("parallel"),
        name="peer_route",
    )(h1, wq, k1, k2)


def _gelu_gate_kernel(act_ref, gate_ref, o_ref):
    a = act_ref[...]
    o_ref[...] = gate_ref[...] * (0.5 * a * (1.0 + lax.erf(a * (2.0 ** -0.5))))


def _gelu_gate(act, gates, tm=2048):
    n, w = act.shape
    blk = pl.BlockSpec((tm, w), lambda i: (i, 0))
    return pl.pallas_call(
        _gelu_gate_kernel,
        grid=(n // tm,),
        in_specs=[blk, blk],
        out_specs=blk,
        out_shape=jax.ShapeDtypeStruct((n, w), F32),
        compiler_params=_cparams("parallel"),
        name="gelu_gate",
    )(act, gates)


def _final_kernel(h_ref, ffn_ref, p_ref, wg_ref, wp_ref, g_ref, b_ref, o_ref):
    h = h_ref[...]
    ple = _sigmoid(_dot(h.astype(BF16), wg_ref[...])) * _dot(p_ref[...].astype(BF16), wp_ref[...])
    o_ref[...] = _layer_norm(ALPHA * h + ffn_ref[...] + ple, g_ref[...], b_ref[...])


def _final(h1, ffn, p2, wg, wp, g, b, tm=512):
    n, d = h1.shape
    const = lambda i: (0, 0)
    rowblk = lambda w: pl.BlockSpec((tm, w), lambda i: (i, 0))
    vec = pl.BlockSpec((1, d), const)
    return pl.pallas_call(
        _final_kernel,
        grid=(n // tm,),
        in_specs=[rowblk(d), rowblk(d), rowblk(p2.shape[1]), pl.BlockSpec(wg.shape, const),
                  pl.BlockSpec(wp.shape, const), vec, vec],
        out_specs=rowblk(d),
        out_shape=jax.ShapeDtypeStruct((n, d), F32),
        compiler_params=_cparams("parallel"),
        name="ple_final_ln",
    )(h1, ffn, p2, wg, wp, g, b)


def _peer_experts_xla(h1, idx, gates, u_tab, v_tab, blk=512):
    n, d = h1.shape
    nb = n // blk

    def one(args):
        hb, ib, gb = args
        ue = jnp.take(u_tab, ib, axis=0)
        act = jnp.einsum('td,tkd->tk', hb, ue)
        w = gb * jax.nn.gelu(act, approximate=False)
        ve = jnp.take(v_tab, ib, axis=0)
        return jnp.einsum('tk,tkd->td', w, ve)

    out = lax.map(one, (h1.reshape(nb, blk, d), idx.reshape(nb, blk, -1), gates.reshape(nb, blk, -1)))
    return out.reshape(n, d)


def _token_mixer_and_norm(x2, b, s, ln0_g, ln0_b, w_in, b_igate, b_fgate, conv_w, conv_b, w_branch_sb, w_branch_ml,
                          w_out, ln1_g, ln1_b):
    n, d = x2.shape
    g0 = ln0_g.reshape(1, d)
    b0 = ln0_b.reshape(1, d)
    n_if = 2 * ML_HEADS
    wa = w_in[:, :PA_WIDTH].astype(BF16)
    wif = jnp.pad(w_in[:, PA_WIDTH:PA_WIDTH + n_if], ((0, 0), (0, LANES - n_if))).astype(BF16)
    wg = w_in[:, PA_WIDTH + n_if:].astype(BF16)
    pa, gif, gate = _inproj(x2, g0, b0, wa, wif, wg)
    pa3 = pa.reshape(b, s, PA_WIDTH)
    o_sb = _sb_attention(pa3)
    qk3 = _conv_silu(pa3, conv_w, conv_b.reshape(1, -1))
    grow = jnp.swapaxes(gif.reshape(b, s, LANES)[:, :, :n_if], 1, 2)
    gbias = jnp.concatenate([b_igate, b_fgate]).reshape(n_if, 1)
    h_ml = _mlstm(qk3, pa3, grow, gbias)
    return _merge(x2, o_sb.reshape(n, SB_WIDTH), h_ml.reshape(n, ML_WIDTH), pa, gate, g0, b0,
                  w_branch_sb.astype(BF16), w_branch_ml.astype(BF16), w_out.astype(BF16),
                  ln1_g.reshape(1, d), ln1_b.reshape(1, d))


def kernel(x, p, ln0_g, ln0_b, w_in, b_igate, b_fgate, conv_w, conv_b, w_branch_sb, w_branch_ml, w_out, ln1_g, ln1_b, peer_wq, peer_k1, peer_k2, peer_u, peer_v, w_ple_gate, w_ple, ln2_g, ln2_b):
    b, s, d = x.shape
    n = b * s
    x2 = x.reshape(n, d)
    i = 0
    h1 = _token_mixer_and_norm(x2, b, s, ln0_g, ln0_b, w_in[i], b_igate[i], b_fgate[i], conv_w[i], conv_b[i],
                               w_branch_sb[i], w_branch_ml[i], w_out[i], ln1_g[i], ln1_b[i])
    idx_t, gate_t = _peer_route(h1, peer_wq[i].astype(BF16), peer_k1[i].astype(BF16), peer_k2[i].astype(BF16))
    idx = idx_t.T
    gates = gate_t.T
    ffn = _peer_experts_xla(h1, idx, gates, peer_u[i], peer_v[i])
    out = _final(h1, ffn, p[i].reshape(n, -1), w_ple_gate[i].astype(BF16), w_ple[i].astype(BF16),
                 ln2_g[i].reshape(1, d), ln2_b[i].reshape(1, d))
    return out.reshape(b, s, d)
```

```python
import functools

import jax
import jax.numpy as jnp
from jax import lax
from jax.experimental import pallas as pl
from jax.experimental.pallas import tpu as pltpu
from jax.experimental.pallas import tpu_sc as plsc

F32 = jnp.float32
BF16 = jnp.bfloat16
I32 = jnp.int32

D_MODEL = 1024
SB_HEADS = 8
SB_HEAD_DIM = 64
SB_WIDTH = SB_HEADS * SB_HEAD_DIM
ML_HEADS = 4
ML_HEAD_DIM = 128
ML_WIDTH = ML_HEADS * ML_HEAD_DIM
ML_CHUNK = 128
CONV_WIDTH = 4
PEER_HEADS = 8
PEER_KEYS = 128
PEER_QDIM = 256
PEER_HALF = PEER_QDIM // 2
PEER_TOPK = 16
PEER_BLOCK = 128
DEPTH = 1
ALPHA = (2.0 * DEPTH) ** 0.25
LN_EPS = 1e-5

LANES = 128
VMEM_LIMIT = 56 * 1024 * 1024
SB_SKIP_LOG = -104.0

PA_WIDTH = 3 * SB_WIDTH + 4 * ML_WIDTH


def _cparams(*sem):
    return pltpu.CompilerParams(dimension_semantics=sem, vmem_limit_bytes=VMEM_LIMIT)


def _layer_norm(x, g, b):
    mu = jnp.mean(x, axis=-1, keepdims=True)
    xc = x - mu
    var = jnp.mean(xc * xc, axis=-1, keepdims=True)
    return xc * lax.rsqrt(var + LN_EPS) * g + b


def _log_sigmoid(z):
    return jnp.minimum(z, 0.0) - jnp.log1p(jnp.exp(-jnp.abs(z)))


def _sigmoid(z):
    return 1.0 / (1.0 + jnp.exp(-z))


def _dot(a, b):
    return jnp.dot(a, b, preferred_element_type=F32)


def _dot_nt(a, b):
    return lax.dot_general(a, b, (((1,), (1,)), ((), ())), preferred_element_type=F32)


def _dot_tn(a, b):
    return lax.dot_general(a, b, (((0,), (0,)), ((), ())), preferred_element_type=F32)


def _inproj_kernel(x_ref, g_ref, b_ref, wa_ref, wif_ref, wg_ref, oa_ref, oif_ref, og_ref, *, cw):
    h = _layer_norm(x_ref[...], g_ref[...], b_ref[...]).astype(BF16)
    for j in range(0, wa_ref.shape[1], cw):
        oa_ref[:, j:j + cw] = _dot(h, wa_ref[:, j:j + cw]).astype(BF16)
    oif_ref[...] = _dot(h, wif_ref[...])
    for j in range(0, wg_ref.shape[1], cw):
        og_ref[:, j:j + cw] = _dot(h, wg_ref[:, j:j + cw]).astype(BF16)


def _inproj(x2, g, b, wa, wif, wg, tm=512):
    n, d = x2.shape
    const = lambda i: (0, 0)
    return pl.pallas_call(
        functools.partial(_inproj_kernel, cw=512),
        grid=(n // tm,),
        in_specs=[
            pl.BlockSpec((tm, d), lambda i: (i, 0)),
            pl.BlockSpec((1, d), const),
            pl.BlockSpec((1, d), const),
            pl.BlockSpec(wa.shape, const),
            pl.BlockSpec(wif.shape, const),
            pl.BlockSpec(wg.shape, const),
        ],
        out_specs=[
            pl.BlockSpec((tm, wa.shape[1]), lambda i: (i, 0)),
            pl.BlockSpec((tm, wif.shape[1]), lambda i: (i, 0)),
            pl.BlockSpec((tm, wg.shape[1]), lambda i: (i, 0)),
        ],
        out_shape=[
            jax.ShapeDtypeStruct((n, wa.shape[1]), BF16),
            jax.ShapeDtypeStruct((n, wif.shape[1]), F32),
            jax.ShapeDtypeStruct((n, wg.shape[1]), BF16),
        ],
        compiler_params=_cparams("parallel"),
        name="ln_inproj",
    )(x2, g, b, wa, wif, wg)


def _sb_kernel(q_ref, k_ref, v_ref, o_ref, *, tq):
    qi = pl.program_id(2)
    q = q_ref[0]
    lane = lax.broadcasted_iota(I32, (1, LANES), 1)
    row = lax.broadcasted_iota(I32, (tq, tq), 0)
    col = lax.broadcasted_iota(I32, (tq, tq), 1)
    causal = col < row
    later = (row > col).astype(BF16)
    scale = SB_HEAD_DIM ** -0.5

    def tile(qh, kb, carry, masked):
        off = pl.multiple_of(kb * tq, tq)
        k_blk = k_ref[0, pl.ds(off, tq), :]
        v_blk = v_ref[0, pl.ds(off, tq), :]
        z = _dot_nt(qh, k_blk)
        lb = _log_sigmoid(z)
        lom = lb - z
        if masked:
            lom = jnp.where(causal, lom, 0.0)
        hi = lom.astype(BF16)
        lo = (lom - hi.astype(F32)).astype(BF16)
        tail = _dot(hi, later) + _dot(lo, later) + carry
        w = jnp.exp(lb + tail)
        if masked:
            w = jnp.where(causal, w, 0.0)
        contrib = _dot(w.astype(BF16), v_blk)
        return contrib, carry + jnp.sum(lom, axis=1, keepdims=True)

    acc_total = jnp.zeros((tq, LANES), F32)
    for hh in range(2):
        hmask = (lane >= SB_HEAD_DIM * hh) & (lane < SB_HEAD_DIM * (hh + 1))
        qh = (jnp.where(hmask, q, jnp.zeros_like(q)).astype(F32) * scale).astype(BF16)
        acc, carry = tile(qh, qi, jnp.zeros((tq, 1), F32), True)

        def cond(st):
            kb, _, _, cmax = st
            return jnp.logical_and(kb >= 0, cmax > SB_SKIP_LOG)

        def body(st):
            kb, acc, carry, _ = st
            contrib, carry = tile(qh, kb, carry, False)
            return kb - 1, acc + contrib, carry, jnp.max(carry)

        _, acc, _, _ = lax.while_loop(cond, body, (qi - 1, acc, carry, jnp.max(carry)))
        acc_total = jnp.where(hmask, acc, acc_total)
    o_ref[0] = acc_total.astype(o_ref.dtype)


def _sb_attention(pa3, tq=128):
    b, s, _ = pa3.shape
    npair = SB_WIDTH // LANES
    return pl.pallas_call(
        functools.partial(_sb_kernel, tq=tq),
        grid=(b, npair, s // tq),
        in_specs=[
            pl.BlockSpec((1, tq, LANES), lambda bi, hp, qi: (bi, qi, hp)),
            pl.BlockSpec((1, s, LANES), lambda bi, hp, qi: (bi, 0, npair + hp)),
            pl.BlockSpec((1, s, LANES), lambda bi, hp, qi: (bi, 0, 2 * npair + hp)),
        ],
        out_specs=pl.BlockSpec((1, tq, LANES), lambda bi, hp, qi: (bi, qi, hp)),
        out_shape=jax.ShapeDtypeStruct((b, s, SB_WIDTH), BF16),
        compiler_params=_cparams("parallel", "parallel", "arbitrary"),
        name="sb_attention",
    )(pa3, pa3, pa3)


CONV_HALO = 16


def _conv_kernel(x_ref, prev_ref, w_ref, b_ref, o_ref, buf_ref, *, ts):
    si = pl.program_id(1)
    is_k = pl.program_id(2)
    prev = prev_ref[0].astype(F32)
    buf_ref[0:CONV_HALO, :] = jnp.where(si == 0, 0.0, prev)
    buf_ref[CONV_HALO:, :] = x_ref[0].astype(F32)
    y = b_ref[...] + w_ref[0:1, :] * buf_ref[CONV_HALO:, :]
    for j in range(1, CONV_WIDTH):
        y = y + w_ref[j:j + 1, :] * buf_ref[CONV_HALO - j:CONV_HALO - j + ts, :]
    y = y * _sigmoid(y)
    y = y * jnp.where(is_k == 1, ML_HEAD_DIM ** -0.5, 1.0)
    o_ref[0] = y.astype(o_ref.dtype)


def _conv_silu(pa3, conv_w, conv_b, ts=1024):
    b, s, _ = pa3.shape
    ts = min(ts, s)
    cb = ML_WIDTH
    base = 3 * SB_WIDTH // cb
    hb = ts // CONV_HALO
    return pl.pallas_call(
        functools.partial(_conv_kernel, ts=ts),
        grid=(b, s // ts, 2),
        in_specs=[
            pl.BlockSpec((1, ts, cb), lambda bi, si, j: (bi, si, base + j)),
            pl.BlockSpec((1, CONV_HALO, cb), lambda bi, si, j: (bi, jnp.maximum(si * hb - 1, 0), base + j)),
            pl.BlockSpec((CONV_WIDTH, cb), lambda bi, si, j: (0, j)),
            pl.BlockSpec((1, cb), lambda bi, si, j: (0, j)),
        ],
        out_specs=pl.BlockSpec((1, ts, cb), lambda bi, si, j: (bi, si, j)),
        out_shape=jax.ShapeDtypeStruct((b, s, 2 * ML_WIDTH), BF16),
        scratch_shapes=[pltpu.VMEM((ts + CONV_HALO, cb), F32)],
        compiler_params=_cparams("parallel", "parallel", "parallel"),
        name="conv_silu",
    )(pa3, pa3, conv_w, conv_b)


def _mlstm_kernel(q_ref, k_ref, v_ref, g_ref, gb_ref, o_ref):
    L = ML_CHUNK
    head = pl.program_id(1)
    nc = q_ref.shape[1] // L
    r = lax.broadcasted_iota(I32, (L, L), 0)
    c = lax.broadcasted_iota(I32, (L, L), 1)
    eye = r == c
    sub = lax.broadcasted_iota(I32, (2 * ML_HEADS, L), 0)

    def to_col(row):
        return jnp.sum(jnp.where(eye, jnp.broadcast_to(row, (L, L)), 0.0), axis=1, keepdims=True)

    def chunk(ci, st):
        c_st, n_st, m_st = st
        off = pl.multiple_of(ci * L, L)
        q = q_ref[0, pl.ds(off, L), :]
        k = k_ref[0, pl.ds(off, L), :]
        v = v_ref[0, pl.ds(off, L), :]
        g = g_ref[0, :, pl.ds(off, L)] + gb_ref[...]
        li_row = jnp.sum(jnp.where(sub == head, g, 0.0), axis=0, keepdims=True)
        lf_row = _log_sigmoid(jnp.sum(jnp.where(sub == head + ML_HEADS, g, 0.0), axis=0, keepdims=True))
        lf_b = jnp.broadcast_to(lf_row, (L, L))
        bcum_col = jnp.sum(jnp.where(c <= r, lf_b, 0.0), axis=1, keepdims=True)
        lf_col = to_col(lf_row)
        li_col = to_col(li_row)
        bcum_row = jnp.sum(jnp.where(r <= c, jnp.broadcast_to(lf_col, (L, L)), 0.0), axis=0, keepdims=True)
        b_last = jnp.sum(lf_row, axis=1, keepdims=True)

        d_log = jnp.where(c <= r, bcum_col - bcum_row + li_row, -jnp.inf)
        inter = bcum_col + m_st
        m_t = jnp.maximum(inter, jnp.max(d_log, axis=1, keepdims=True))
        a_t = jnp.exp(inter - m_t)
        s_w = _dot_nt(q, k) * jnp.exp(d_log - m_t)
        num = a_t * _dot_nt(q, c_st.astype(BF16)) + _dot(s_w.astype(BF16), v)
        den = a_t * jnp.sum(q.astype(F32) * n_st, axis=1, keepdims=True) + jnp.sum(s_w, axis=1, keepdims=True)
        o_ref[0, pl.ds(off, L), :] = (num / jnp.maximum(jnp.abs(den), jnp.exp(-m_t))).astype(o_ref.dtype)

        w_end = b_last - bcum_col + li_col
        m_loc = jnp.max(w_end, axis=0, keepdims=True)
        e_end = jnp.exp(w_end - m_loc)
        c_loc = _dot_tn((e_end * v.astype(F32)).astype(BF16), k)
        n_loc = jnp.sum(e_end * k.astype(F32), axis=0, keepdims=True)
        m_new = jnp.maximum(b_last + m_st, m_loc)
        a = jnp.exp(b_last + m_st - m_new)
        gg = jnp.exp(m_loc - m_new)
        return a * c_st + gg * c_loc, a * n_st + gg * n_loc, m_new

    init = (jnp.zeros((ML_HEAD_DIM, ML_HEAD_DIM), F32), jnp.zeros((1, ML_HEAD_DIM), F32), jnp.zeros((1, 1), F32))
    lax.fori_loop(0, nc, chunk, init)


def _mlstm(qk3, pa3, grow, gbias):
    b, s, _ = pa3.shape
    vbase = (3 * SB_WIDTH + 2 * ML_WIDTH) // ML_HEAD_DIM
    return pl.pallas_call(
        _mlstm_kernel,
        grid=(b, ML_HEADS),
        in_specs=[
            pl.BlockSpec((1, s, ML_HEAD_DIM), lambda bi, h: (bi, 0, h)),
            pl.BlockSpec((1, s, ML_HEAD_DIM), lambda bi, h: (bi, 0, ML_HEADS + h)),
            pl.BlockSpec((1, s, ML_HEAD_DIM), lambda bi, h: (bi, 0, vbase + h)),
            pl.BlockSpec((1, 2 * ML_HEADS, s), lambda bi, h: (bi, 0, 0)),
            pl.BlockSpec((2 * ML_HEADS, 1), lambda bi, h: (0, 0)),
        ],
        out_specs=pl.BlockSpec((1, s, ML_HEAD_DIM), lambda bi, h: (bi, 0, h)),
        out_shape=jax.ShapeDtypeStruct((b, s, ML_WIDTH), BF16),
        compiler_params=_cparams("parallel", "parallel"),
        name="mlstm",
    )(qk3, qk3, pa3, grow, gbias)


def _merge_kernel(x_ref, osb_ref, hml_ref, mlo_ref, gate_ref, g0_ref, b0_ref, wsb_ref, wml_ref, wout_ref,
                  g1_ref, b1_ref, o_ref):
    h0 = _layer_norm(x_ref[...], g0_ref[...], b0_ref[...])
    o_ml = (_sigmoid(mlo_ref[...].astype(F32)) * hml_ref[...].astype(F32)).astype(BF16)
    y = _sigmoid(gate_ref[:, :D_MODEL].astype(F32)) * _dot(osb_ref[...], wsb_ref[...])
    y = y + _sigmoid(gate_ref[:, D_MODEL:].astype(F32)) * _dot(o_ml, wml_ref[...])
    mix = _dot(y.astype(BF16), wout_ref[...])
    o_ref[...] = _layer_norm(ALPHA * h0 + mix, g1_ref[...], b1_ref[...])


def _merge(x2, osb, hml, pa, gate, g0, b0, wsb, wml, wout, g1, b1, tm=512):
    n, d = x2.shape
    const = lambda i: (0, 0)
    rowblk = lambda w: pl.BlockSpec((tm, w), lambda i: (i, 0))
    vec = pl.BlockSpec((1, d), const)
    mlo_blk = (3 * SB_WIDTH + 3 * ML_WIDTH) // ML_WIDTH
    return pl.pallas_call(
        _merge_kernel,
        grid=(n // tm,),
        in_specs=[
            rowblk(d), rowblk(SB_WIDTH), rowblk(ML_WIDTH),
            pl.BlockSpec((tm, ML_WIDTH), lambda i: (i, mlo_blk)),
            rowblk(2 * d), vec, vec,
            pl.BlockSpec(wsb.shape, const), pl.BlockSpec(wml.shape, const), pl.BlockSpec(wout.shape, const),
            vec, vec,
        ],
        out_specs=rowblk(d),
        out_shape=jax.ShapeDtypeStruct((n, d), F32),
        compiler_params=_cparams("parallel"),
        name="merge_outproj_ln",
    )(x2, osb, hml, pa, gate, g0, b0, wsb, wml, wout, g1, b1)


def _topk_rows(s, k):
    n = s.shape[0]
    iota = lax.broadcasted_iota(I32, s.shape, 0)
    vals, idxs = [], []
    for _ in range(k):
        m = jnp.max(s, axis=0, keepdims=True)
        am = jnp.min(jnp.where(s == m, iota, n), axis=0, keepdims=True)
        vals.append(m)
        idxs.append(am)
        s = jnp.where(iota == am, -jnp.inf, s)
    return jnp.concatenate(vals, axis=0), jnp.concatenate(idxs, axis=0)


def _select_rows(sel, table):
    out = jnp.zeros(sel.shape, table.dtype)
    for r_ in range(table.shape[0]):
        out = jnp.where(sel == r_, table[r_:r_ + 1, :], out)
    return out


def _peer_route_kernel(h_ref, wq_ref, k1_ref, k2_ref, idx_ref, gate_ref, q_scr, *, tt):
    q_scr[...] = _dot(h_ref[...].astype(BF16), wq_ref[...]).astype(BF16)
    nsub = h_ref.shape[0] // tt

    def one(it, _):
        head = it % PEER_HEADS
        sub = it // PEER_HEADS
        roff = pl.multiple_of(sub * tt, tt)
        coff = pl.multiple_of(head * PEER_QDIM, PEER_QDIM)
        q1 = q_scr[pl.ds(roff, tt), pl.ds(coff, PEER_HALF)]
        q2 = q_scr[pl.ds(roff, tt), pl.ds(coff + PEER_HALF, PEER_HALF)]
        v1, i1 = _topk_rows(_dot_nt(k1_ref[...], q1), PEER_TOPK)
        v2, i2 = _topk_rows(_dot_nt(k2_ref[...], q2), PEER_TOPK)
        cand = jnp.concatenate([v1[i:i + 1, :] + v2 for i in range(PEER_TOPK)], axis=0)
        top_s, pos = _topk_rows(cand, PEER_TOPK)
        e1 = _select_rows(pos // PEER_TOPK, i1)
        e2 = _select_rows(pos % PEER_TOPK, i2)
        ex = jnp.exp(top_s - top_s[0:1, :])
        gates = ex / jnp.sum(ex, axis=0, keepdims=True)
        hoff = pl.multiple_of(head * PEER_TOPK, PEER_TOPK)
        idx_ref[pl.ds(hoff, PEER_TOPK), pl.ds(roff, tt)] = e1 * PEER_KEYS + e2
        gate_ref[pl.ds(hoff, PEER_TOPK), pl.ds(roff, tt)] = gates
        return 0

    lax.fori_loop(0, nsub * PEER_HEADS, one, 0)


def _peer_route(h1, wq, k1, k2, tm=256, tt=128):
    n, d = h1.shape
    const = lambda i: (0, 0)
    nsel = PEER_HEADS * PEER_TOPK
    return pl.pallas_call(
        functools.partial(_peer_route_kernel, tt=tt),
        grid=(n // tm,),
        in_specs=[
            pl.BlockSpec((tm, d), lambda i: (i, 0)),
            pl.BlockSpec(wq.shape, const),
            pl.BlockSpec(k1.shape, const),
            pl.BlockSpec(k2.shape, const),
        ],
        out_specs=[pl.BlockSpec((nsel, tm), lambda i: (0, i)), pl.BlockSpec((nsel, tm), lambda i: (0, i))],
        out_shape=[jax.ShapeDtypeStruct((nsel, n), I32), jax.ShapeDtypeStruct((nsel, n), F32)],
        scratch_shapes=[pltpu.VMEM((tm, PEER_HEADS * PEER_QDIM), BF16)],
        compiler_params=_cparams("parallel"),
        name="peer_route",
    )(h1, wq, k1, k2)


def _dot_split3(x, sel):
    hi = x.astype(BF16)
    r1 = x - hi.astype(F32)
    mid = r1.astype(BF16)
    lo = (r1 - mid.astype(F32)).astype(BF16)
    return _dot(hi, sel) + _dot(mid, sel) + _dot(lo, sel)


def _gelu_gate_kernel(part_ref, gate_ref, sel_ref, selt_ref, o_ref):
    a = _dot_split3(part_ref[...], sel_ref[...])
    w = gate_ref[...] * (0.5 * a * (1.0 + lax.erf(a * (2.0 ** -0.5))))
    o_ref[...] = _dot_split3(w, selt_ref[...])


def _gelu_gate(part, gates, tm=512):
    n, wide = part.shape
    nsel = gates.shape[1]
    lanes = wide // nsel
    sel = (jnp.arange(wide)[:, None] // lanes == jnp.arange(nsel)[None, :]).astype(BF16)
    const = lambda i: (0, 0)
    return pl.pallas_call(
        _gelu_gate_kernel,
        grid=(n // tm,),
        in_specs=[pl.BlockSpec((tm, wide), lambda i: (i, 0)), pl.BlockSpec((tm, nsel), lambda i: (i, 0)),
                  pl.BlockSpec((wide, nsel), const), pl.BlockSpec((nsel, wide), const)],
        out_specs=pl.BlockSpec((tm, wide), lambda i: (i, 0)),
        out_shape=jax.ShapeDtypeStruct((n, wide), F32),
        compiler_params=_cparams("parallel"),
        name="gelu_gate",
    )(part, gates, sel, sel.T)


def _final_kernel(h_ref, ffn_ref, p_ref, wg_ref, wp_ref, g_ref, b_ref, o_ref):
    h = h_ref[...]
    ple = _sigmoid(_dot(h.astype(BF16), wg_ref[...])) * _dot(p_ref[...].astype(BF16), wp_ref[...])
    o_ref[...] = _layer_norm(ALPHA * h + ffn_ref[...] + ple, g_ref[...], b_ref[...])


def _final(h1, ffn, p2, wg, wp, g, b, tm=512):
    n, d = h1.shape
    const = lambda i: (0, 0)
    rowblk = lambda w: pl.BlockSpec((tm, w), lambda i: (i, 0))
    vec = pl.BlockSpec((1, d), const)
    return pl.pallas_call(
        _final_kernel,
        grid=(n // tm,),
        in_specs=[rowblk(d), rowblk(d), rowblk(p2.shape[1]), pl.BlockSpec(wg.shape, const),
                  pl.BlockSpec(wp.shape, const), vec, vec],
        out_specs=rowblk(d),
        out_shape=jax.ShapeDtypeStruct((n, d), F32),
        compiler_params=_cparams("parallel"),
        name="ple_final_ln",
    )(h1, ffn, p2, wg, wp, g, b)


SC_CORES = 2
SC_SUBCORES = 16
SC_LANES = 16
SC_WORKERS = SC_CORES * SC_SUBCORES
PEER_NSEL = PEER_HEADS * PEER_TOPK
SC_ROWS = 32
SC_GROUP = 8
SC_COLS = 256


def _sc_mesh():
    return plsc.VectorSubcoreMesh(core_axis_name="c", subcore_axis_name="s", num_cores=SC_CORES,
                                  num_subcores=SC_SUBCORES)


def _sc_token_pipeline(tab_hbm, idx_v, buf, sems, compute):
    nchunk = PEER_NSEL // SC_ROWS
    nsteps = SC_GROUP * nchunk

    def gather(step, slot):
        t = step // nchunk
        c = step % nchunk
        return pltpu.make_async_copy(tab_hbm.at[idx_v.at[t, pl.ds(c * SC_ROWS, SC_ROWS)]], buf.at[slot],
                                     sems.at[slot])

    gather(0, 0).start()

    def pair(sp, _):
        s0 = 2 * sp
        gather(s0 + 1, 1).start()
        gather(s0, 0).wait()
        compute(s0 // nchunk, s0 % nchunk, 0)

        @pl.when(sp < nsteps // 2 - 1)
        def _():
            gather(s0 + 2, 0).start()

        gather(s0 + 1, 1).wait()
        compute((s0 + 1) // nchunk, (s0 + 1) % nchunk, 1)
        return 0

    lax.fori_loop(0, nsteps // 2, pair, 0)


def _peer_act_sc(h1, idx, u_tab):
    n, d = h1.shape
    per_w = n // SC_WORKERS
    ncol = SC_COLS // SC_LANES

    @functools.partial(
        pl.kernel, mesh=_sc_mesh(),
        out_type=jax.ShapeDtypeStruct((n, PEER_NSEL * SC_LANES), F32),
        scratch_types=[
            pltpu.VMEM((SC_GROUP, d), F32),
            pltpu.VMEM((SC_GROUP, PEER_NSEL), I32),
            pltpu.VMEM((SC_GROUP, PEER_NSEL * SC_LANES), F32),
            pltpu.VMEM((2, SC_ROWS, d), F32),
            pltpu.SemaphoreType.DMA((2,)),
        ],
        name="peer_act_sc",
    )
    def k(h_hbm, idx_hbm, u_hbm, out_hbm, h_v, idx_v, out_v, buf, sems):
        base = (lax.axis_index("s") * SC_CORES + lax.axis_index("c")) * per_w

        def compute(t, c, slot):
            def row_group(rg, _):
                row0 = rg * SC_LANES
                accs = [jnp.zeros((SC_LANES,), F32) for _ in range(SC_LANES)]
                for cb in range(d // SC_COLS):
                    col0 = cb * SC_COLS
                    hv = [h_v[t, pl.ds(col0 + j * SC_LANES, SC_LANES)] for j in range(ncol)]
                    for r in range(SC_LANES):
                        for j in range(ncol):
                            accs[r] = accs[r] + buf[slot, row0 + r, pl.ds(col0 + j * SC_LANES, SC_LANES)] * hv[j]
                for r in range(SC_LANES):
                    out_v[t, pl.ds((c * SC_ROWS + row0 + r) * SC_LANES, SC_LANES)] = accs[r]
                return 0

            lax.fori_loop(0, SC_ROWS // SC_LANES, row_group, 0)

        def group(g, _):
            tok0 = base + g * SC_GROUP
            pltpu.sync_copy(h_hbm.at[pl.ds(tok0, SC_GROUP)], h_v)
            pltpu.sync_copy(idx_hbm.at[pl.ds(tok0, SC_GROUP)], idx_v)
            _sc_token_pipeline(u_hbm, idx_v, buf, sems, compute)
            pltpu.sync_copy(out_v, out_hbm.at[pl.ds(tok0, SC_GROUP)])
            return 0

        lax.fori_loop(0, per_w // SC_GROUP, group, 0)

    return k(h1, idx, u_tab)


def _peer_combine_sc(wexp, idx, v_tab):
    n = idx.shape[0]
    d = v_tab.shape[1]
    per_w = n // SC_WORKERS
    ncol = SC_COLS // SC_LANES
    nvec = d // SC_LANES

    @functools.partial(
        pl.kernel, mesh=_sc_mesh(),
        out_type=jax.ShapeDtypeStruct((n, d), F32),
        scratch_types=[
            pltpu.VMEM((SC_GROUP, PEER_NSEL * SC_LANES), F32),
            pltpu.VMEM((SC_GROUP, PEER_NSEL), I32),
            pltpu.VMEM((SC_GROUP, d), F32),
            pltpu.VMEM((2, SC_ROWS, d), F32),
            pltpu.SemaphoreType.DMA((2,)),
        ],
        name="peer_combine_sc",
    )
    def k(w_hbm, idx_hbm, v_hbm, out_hbm, w_v, idx_v, o_v, buf, sems):
        base = (lax.axis_index("s") * SC_CORES + lax.axis_index("c")) * per_w

        def compute(t, c, slot):
            def col_block(cb, _):
                col0 = cb * SC_COLS
                acc = [o_v[t, pl.ds(col0 + j * SC_LANES, SC_LANES)] for j in range(ncol)]
                for r in range(SC_ROWS):
                    w = w_v[t, pl.ds((c * SC_ROWS + r) * SC_LANES, SC_LANES)]
                    for j in range(ncol):
                        acc[j] = acc[j] + w * buf[slot, r, pl.ds(col0 + j * SC_LANES, SC_LANES)]
                for j in range(ncol):
                    o_v[t, pl.ds(col0 + j * SC_LANES, SC_LANES)] = acc[j]
                return 0

            lax.fori_loop(0, d // SC_COLS, col_block, 0)

        def group(g, _):
            tok0 = base + g * SC_GROUP
            pltpu.sync_copy(w_hbm.at[pl.ds(tok0, SC_GROUP)], w_v)
            pltpu.sync_copy(idx_hbm.at[pl.ds(tok0, SC_GROUP)], idx_v)

            def zero(i, _):
                o_v[i // nvec, pl.ds((i % nvec) * SC_LANES, SC_LANES)] = jnp.zeros((SC_LANES,), F32)
                return 0

            lax.fori_loop(0, SC_GROUP * nvec, zero, 0)
            _sc_token_pipeline(v_hbm, idx_v, buf, sems, compute)
            pltpu.sync_copy(o_v, out_hbm.at[pl.ds(tok0, SC_GROUP)])
            return 0

        lax.fori_loop(0, per_w // SC_GROUP, group, 0)

    return k(wexp, idx, v_tab)


def _token_mixer_and_norm(x2, b, s, ln0_g, ln0_b, w_in, b_igate, b_fgate, conv_w, conv_b, w_branch_sb, w_branch_ml,
                          w_out, ln1_g, ln1_b):
    n, d = x2.shape
    g0 = ln0_g.reshape(1, d)
    b0 = ln0_b.reshape(1, d)
    n_if = 2 * ML_HEADS
    wa = w_in[:, :PA_WIDTH].astype(BF16)
    wif = jnp.pad(w_in[:, PA_WIDTH:PA_WIDTH + n_if], ((0, 0), (0, LANES - n_if))).astype(BF16)
    wg = w_in[:, PA_WIDTH + n_if:].astype(BF16)
    pa, gif, gate = _inproj(x2, g0, b0, wa, wif, wg)
    pa3 = pa.reshape(b, s, PA_WIDTH)
    o_sb = _sb_attention(pa3)
    qk3 = _conv_silu(pa3, conv_w, conv_b.reshape(1, -1))
    grow = jnp.swapaxes(gif.reshape(b, s, LANES)[:, :, :n_if], 1, 2)
    gbias = jnp.concatenate([b_igate, b_fgate]).reshape(n_if, 1)
    h_ml = _mlstm(qk3, pa3, grow, gbias)
    return _merge(x2, o_sb.reshape(n, SB_WIDTH), h_ml.reshape(n, ML_WIDTH), pa, gate, g0, b0,
                  w_branch_sb.astype(BF16), w_branch_ml.astype(BF16), w_out.astype(BF16),
                  ln1_g.reshape(1, d), ln1_b.reshape(1, d))


def kernel(x, p, ln0_g, ln0_b, w_in, b_igate, b_fgate, conv_w, conv_b, w_branch_sb, w_branch_ml, w_out, ln1_g, ln1_b, peer_wq, peer_k1, peer_k2, peer_u, peer_v, w_ple_gate, w_ple, ln2_g, ln2_b):
    b, s, d = x.shape
    n = b * s
    x2 = x.reshape(n, d)
    i = 0
    h1 = _token_mixer_and_norm(x2, b, s, ln0_g, ln0_b, w_in[i], b_igate[i], b_fgate[i], conv_w[i], conv_b[i],
                               w_branch_sb[i], w_branch_ml[i], w_out[i], ln1_g[i], ln1_b[i])
    idx_t, gate_t = _peer_route(h1, peer_wq[i].astype(BF16), peer_k1[i].astype(BF16), peer_k2[i].astype(BF16))
    idx = idx_t.T
    gates = gate_t.T
    part = _peer_act_sc(h1, idx, peer_u[i])
    wexp = _gelu_gate(part, gates)
    ffn = _peer_combine_sc(wexp, idx, peer_v[i])
    out = _final(h1, ffn, p[i].reshape(n, -1), w_ple_gate[i].astype(BF16), w_ple[i].astype(BF16),
                 ln2_g[i].reshape(1, d), ln2_b[i].reshape(1, d))
    return out.reshape(b, s, d)
```

```python
import functools

import jax
import jax.numpy as jnp
from jax import lax
from jax.experimental import pallas as pl
from jax.experimental.pallas import tpu as pltpu
from jax.experimental.pallas import tpu_sc as plsc

F32 = jnp.float32
BF16 = jnp.bfloat16
I32 = jnp.int32

D_MODEL = 1024
SB_HEADS = 8
SB_HEAD_DIM = 64
SB_WIDTH = SB_HEADS * SB_HEAD_DIM
ML_HEADS = 4
ML_HEAD_DIM = 128
ML_WIDTH = ML_HEADS * ML_HEAD_DIM
ML_CHUNK = 128
CONV_WIDTH = 4
PEER_HEADS = 8
PEER_KEYS = 128
PEER_QDIM = 256
PEER_HALF = PEER_QDIM // 2
PEER_TOPK = 16
PEER_BLOCK = 128
DEPTH = 1
ALPHA = (2.0 * DEPTH) ** 0.25
LN_EPS = 1e-5

LANES = 128
VMEM_LIMIT = 56 * 1024 * 1024
SB_SKIP_LOG = -104.0

PA_WIDTH = 3 * SB_WIDTH + 4 * ML_WIDTH


def _cparams(*sem):
    return pltpu.CompilerParams(dimension_semantics=sem, vmem_limit_bytes=VMEM_LIMIT)


def _layer_norm(x, g, b):
    mu = jnp.mean(x, axis=-1, keepdims=True)
    xc = x - mu
    var = jnp.mean(xc * xc, axis=-1, keepdims=True)
    return xc * lax.rsqrt(var + LN_EPS) * g + b


def _log_sigmoid(z):
    return jnp.minimum(z, 0.0) - jnp.log1p(jnp.exp(-jnp.abs(z)))


def _sigmoid(z):
    return 1.0 / (1.0 + jnp.exp(-z))


def _dot(a, b):
    return jnp.dot(a, b, preferred_element_type=F32)


def _dot_nt(a, b):
    return lax.dot_general(a, b, (((1,), (1,)), ((), ())), preferred_element_type=F32)


def _dot_tn(a, b):
    return lax.dot_general(a, b, (((0,), (0,)), ((), ())), preferred_element_type=F32)


def _inproj_kernel(x_ref, g_ref, b_ref, wa_ref, wif_ref, wg_ref, oa_ref, oif_ref, og_ref, *, cw):
    h = _layer_norm(x_ref[...], g_ref[...], b_ref[...]).astype(BF16)
    for j in range(0, wa_ref.shape[1], cw):
        oa_ref[:, j:j + cw] = _dot(h, wa_ref[:, j:j + cw]).astype(BF16)
    oif_ref[...] = _dot(h, wif_ref[...])
    for j in range(0, wg_ref.shape[1], cw):
        og_ref[:, j:j + cw] = _dot(h, wg_ref[:, j:j + cw]).astype(BF16)


def _inproj(x2, g, b, wa, wif, wg, tm=512):
    n, d = x2.shape
    const = lambda i: (0, 0)
    return pl.pallas_call(
        functools.partial(_inproj_kernel, cw=512),
        grid=(n // tm,),
        in_specs=[
            pl.BlockSpec((tm, d), lambda i: (i, 0)),
            pl.BlockSpec((1, d), const),
            pl.BlockSpec((1, d), const),
            pl.BlockSpec(wa.shape, const),
            pl.BlockSpec(wif.shape, const),
            pl.BlockSpec(wg.shape, const),
        ],
        out_specs=[
            pl.BlockSpec((tm, wa.shape[1]), lambda i: (i, 0)),
            pl.BlockSpec((tm, wif.shape[1]), lambda i: (i, 0)),
            pl.BlockSpec((tm, wg.shape[1]), lambda i: (i, 0)),
        ],
        out_shape=[
            jax.ShapeDtypeStruct((n, wa.shape[1]), BF16),
            jax.ShapeDtypeStruct((n, wif.shape[1]), F32),
            jax.ShapeDtypeStruct((n, wg.shape[1]), BF16),
        ],
        compiler_params=_cparams("parallel"),
        name="ln_inproj",
    )(x2, g, b, wa, wif, wg)


def _sb_kernel(q_ref, k_ref, v_ref, o_ref, *, tq):
    qi = pl.program_id(2)
    q = q_ref[0]
    lane = lax.broadcasted_iota(I32, (1, LANES), 1)
    row = lax.broadcasted_iota(I32, (tq, tq), 0)
    col = lax.broadcasted_iota(I32, (tq, tq), 1)
    causal = col < row
    later = (row > col).astype(BF16)
    scale = SB_HEAD_DIM ** -0.5

    def tile(qh, kb, carry, masked):
        off = pl.multiple_of(kb * tq, tq)
        k_blk = k_ref[0, pl.ds(off, tq), :]
        v_blk = v_ref[0, pl.ds(off, tq), :]
        z = _dot_nt(qh, k_blk)
        lb = _log_sigmoid(z)
        lom = lb - z
        if masked:
            lom = jnp.where(causal, lom, 0.0)
        hi = lom.astype(BF16)
        lo = (lom - hi.astype(F32)).astype(BF16)
        tail = _dot(hi, later) + _dot(lo, later) + carry
        w = jnp.exp(lb + tail)
        if masked:
            w = jnp.where(causal, w, 0.0)
        contrib = _dot(w.astype(BF16), v_blk)
        return contrib, carry + jnp.sum(lom, axis=1, keepdims=True)

    acc_total = jnp.zeros((tq, LANES), F32)
    for hh in range(2):
        hmask = (lane >= SB_HEAD_DIM * hh) & (lane < SB_HEAD_DIM * (hh + 1))
        qh = (jnp.where(hmask, q, jnp.zeros_like(q)).astype(F32) * scale).astype(BF16)
        acc, carry = tile(qh, qi, jnp.zeros((tq, 1), F32), True)

        def cond(st):
            kb, _, _, cmax = st
            return jnp.logical_and(kb >= 0, cmax > SB_SKIP_LOG)

        def body(st):
            kb, acc, carry, _ = st
            contrib, carry = tile(qh, kb, carry, False)
            return kb - 1, acc + contrib, carry, jnp.max(carry)

        _, acc, _, _ = lax.while_loop(cond, body, (qi - 1, acc, carry, jnp.max(carry)))
        acc_total = jnp.where(hmask, acc, acc_total)
    o_ref[0] = acc_total.astype(o_ref.dtype)


def _sb_attention(pa3, tq=128):
    b, s, _ = pa3.shape
    npair = SB_WIDTH // LANES
    return pl.pallas_call(
        functools.partial(_sb_kernel, tq=tq),
        grid=(b, npair, s // tq),
        in_specs=[
            pl.BlockSpec((1, tq, LANES), lambda bi, hp, qi: (bi, qi, hp)),
            pl.BlockSpec((1, s, LANES), lambda bi, hp, qi: (bi, 0, npair + hp)),
            pl.BlockSpec((1, s, LANES), lambda bi, hp, qi: (bi, 0, 2 * npair + hp)),
        ],
        out_specs=pl.BlockSpec((1, tq, LANES), lambda bi, hp, qi: (bi, qi, hp)),
        out_shape=jax.ShapeDtypeStruct((b, s, SB_WIDTH), BF16),
        compiler_params=_cparams("parallel", "parallel", "arbitrary"),
        name="sb_attention",
    )(pa3, pa3, pa3)


CONV_HALO = 16


def _conv_kernel(x_ref, prev_ref, w_ref, b_ref, o_ref, buf_ref, *, ts):
    si = pl.program_id(1)
    is_k = pl.program_id(2)
    prev = prev_ref[0].astype(F32)
    buf_ref[0:CONV_HALO, :] = jnp.where(si == 0, 0.0, prev)
    buf_ref[CONV_HALO:, :] = x_ref[0].astype(F32)
    y = b_ref[...] + w_ref[0:1, :] * buf_ref[CONV_HALO:, :]
    for j in range(1, CONV_WIDTH):
        y = y + w_ref[j:j + 1, :] * buf_ref[CONV_HALO - j:CONV_HALO - j + ts, :]
    y = y * _sigmoid(y)
    y = y * jnp.where(is_k == 1, ML_HEAD_DIM ** -0.5, 1.0)
    o_ref[0] = y.astype(o_ref.dtype)


def _conv_silu(pa3, conv_w, conv_b, ts=1024):
    b, s, _ = pa3.shape
    ts = min(ts, s)
    cb = ML_WIDTH
    base = 3 * SB_WIDTH // cb
    hb = ts // CONV_HALO
    return pl.pallas_call(
        functools.partial(_conv_kernel, ts=ts),
        grid=(b, s // ts, 2),
        in_specs=[
            pl.BlockSpec((1, ts, cb), lambda bi, si, j: (bi, si, base + j)),
            pl.BlockSpec((1, CONV_HALO, cb), lambda bi, si, j: (bi, jnp.maximum(si * hb - 1, 0), base + j)),
            pl.BlockSpec((CONV_WIDTH, cb), lambda bi, si, j: (0, j)),
            pl.BlockSpec((1, cb), lambda bi, si, j: (0, j)),
        ],
        out_specs=pl.BlockSpec((1, ts, cb), lambda bi, si, j: (bi, si, j)),
        out_shape=jax.ShapeDtypeStruct((b, s, 2 * ML_WIDTH), BF16),
        scratch_shapes=[pltpu.VMEM((ts + CONV_HALO, cb), F32)],
        compiler_params=_cparams("parallel", "parallel", "parallel"),
        name="conv_silu",
    )(pa3, pa3, conv_w, conv_b)


def _mlstm_kernel(q_ref, k_ref, v_ref, g_ref, gb_ref, o_ref):
    L = ML_CHUNK
    head = pl.program_id(1)
    nc = q_ref.shape[1] // L
    r = lax.broadcasted_iota(I32, (L, L), 0)
    c = lax.broadcasted_iota(I32, (L, L), 1)
    eye = r == c
    sub = lax.broadcasted_iota(I32, (2 * ML_HEADS, L), 0)

    def to_col(row):
        return jnp.sum(jnp.where(eye, jnp.broadcast_to(row, (L, L)), 0.0), axis=1, keepdims=True)

    def chunk(ci, st):
        c_st, n_st, m_st = st
        off = pl.multiple_of(ci * L, L)
        q = q_ref[0, pl.ds(off, L), :]
        k = k_ref[0, pl.ds(off, L), :]
        v = v_ref[0, pl.ds(off, L), :]
        g = g_ref[0, :, pl.ds(off, L)] + gb_ref[...]
        li_row = jnp.sum(jnp.where(sub == head, g, 0.0), axis=0, keepdims=True)
        lf_row = _log_sigmoid(jnp.sum(jnp.where(sub == head + ML_HEADS, g, 0.0), axis=0, keepdims=True))
        lf_b = jnp.broadcast_to(lf_row, (L, L))
        bcum_col = jnp.sum(jnp.where(c <= r, lf_b, 0.0), axis=1, keepdims=True)
        lf_col = to_col(lf_row)
        li_col = to_col(li_row)
        bcum_row = jnp.sum(jnp.where(r <= c, jnp.broadcast_to(lf_col, (L, L)), 0.0), axis=0, keepdims=True)
        b_last = jnp.sum(lf_row, axis=1, keepdims=True)

        d_log = jnp.where(c <= r, bcum_col - bcum_row + li_row, -jnp.inf)
        inter = bcum_col + m_st
        m_t = jnp.maximum(inter, jnp.max(d_log, axis=1, keepdims=True))
        a_t = jnp.exp(inter - m_t)
        s_w = _dot_nt(q, k) * jnp.exp(d_log - m_t)
        num = a_t * _dot_nt(q, c_st.astype(BF16)) + _dot(s_w.astype(BF16), v)
        den = a_t * jnp.sum(q.astype(F32) * n_st, axis=1, keepdims=True) + jnp.sum(s_w, axis=1, keepdims=True)
        o_ref[0, pl.ds(off, L), :] = (num / jnp.maximum(jnp.abs(den), jnp.exp(-m_t))).astype(o_ref.dtype)

        w_end = b_last - bcum_col + li_col
        m_loc = jnp.max(w_end, axis=0, keepdims=True)
        e_end = jnp.exp(w_end - m_loc)
        c_loc = _dot_tn((e_end * v.astype(F32)).astype(BF16), k)
        n_loc = jnp.sum(e_end * k.astype(F32), axis=0, keepdims=True)
        m_new = jnp.maximum(b_last + m_st, m_loc)
        a = jnp.exp(b_last + m_st - m_new)
        gg = jnp.exp(m_loc - m_new)
        return a * c_st + gg * c_loc, a * n_st + gg * n_loc, m_new

    init = (jnp.zeros((ML_HEAD_DIM, ML_HEAD_DIM), F32), jnp.zeros((1, ML_HEAD_DIM), F32), jnp.zeros((1, 1), F32))
    lax.fori_loop(0, nc, chunk, init)


def _mlstm(qk3, pa3, grow, gbias):
    b, s, _ = pa3.shape
    vbase = (3 * SB_WIDTH + 2 * ML_WIDTH) // ML_HEAD_DIM
    return pl.pallas_call(
        _mlstm_kernel,
        grid=(b, ML_HEADS),
        in_specs=[
            pl.BlockSpec((1, s, ML_HEAD_DIM), lambda bi, h: (bi, 0, h)),
            pl.BlockSpec((1, s, ML_HEAD_DIM), lambda bi, h: (bi, 0, ML_HEADS + h)),
            pl.BlockSpec((1, s, ML_HEAD_DIM), lambda bi, h: (bi, 0, vbase + h)),
            pl.BlockSpec((1, 2 * ML_HEADS, s), lambda bi, h: (bi, 0, 0)),
            pl.BlockSpec((2 * ML_HEADS, 1), lambda bi, h: (0, 0)),
        ],
        out_specs=pl.BlockSpec((1, s, ML_HEAD_DIM), lambda bi, h: (bi, 0, h)),
        out_shape=jax.ShapeDtypeStruct((b, s, ML_WIDTH), BF16),
        compiler_params=_cparams("parallel", "parallel"),
        name="mlstm",
    )(qk3, qk3, pa3, grow, gbias)


def _merge_kernel(x_ref, osb_ref, hml_ref, mlo_ref, gate_ref, g0_ref, b0_ref, wsb_ref, wml_ref, wout_ref,
                  g1_ref, b1_ref, o_ref):
    h0 = _layer_norm(x_ref[...], g0_ref[...], b0_ref[...])
    o_ml = (_sigmoid(mlo_ref[...].astype(F32)) * hml_ref[...].astype(F32)).astype(BF16)
    y = _sigmoid(gate_ref[:, :D_MODEL].astype(F32)) * _dot(osb_ref[...], wsb_ref[...])
    y = y + _sigmoid(gate_ref[:, D_MODEL:].astype(F32)) * _dot(o_ml, wml_ref[...])
    mix = _dot(y.astype(BF16), wout_ref[...])
    o_ref[...] = _layer_norm(ALPHA * h0 + mix, g1_ref[...], b1_ref[...])


def _merge(x2, osb, hml, pa, gate, g0, b0, wsb, wml, wout, g1, b1, tm=512):
    n, d = x2.shape
    const = lambda i: (0, 0)
    rowblk = lambda w: pl.BlockSpec((tm, w), lambda i: (i, 0))
    vec = pl.BlockSpec((1, d), const)
    mlo_blk = (3 * SB_WIDTH + 3 * ML_WIDTH) // ML_WIDTH
    return pl.pallas_call(
        _merge_kernel,
        grid=(n // tm,),
        in_specs=[
            rowblk(d), rowblk(SB_WIDTH), rowblk(ML_WIDTH),
            pl.BlockSpec((tm, ML_WIDTH), lambda i: (i, mlo_blk)),
            rowblk(2 * d), vec, vec,
            pl.BlockSpec(wsb.shape, const), pl.BlockSpec(wml.shape, const), pl.BlockSpec(wout.shape, const),
            vec, vec,
        ],
        out_specs=rowblk(d),
        out_shape=jax.ShapeDtypeStruct((n, d), F32),
        compiler_params=_cparams("parallel"),
        name="merge_outproj_ln",
    )(x2, osb, hml, pa, gate, g0, b0, wsb, wml, wout, g1, b1)


def _topk_rows(s, k):
    n = s.shape[0]
    iota = lax.broadcasted_iota(I32, s.shape, 0)
    vals, idxs = [], []
    for _ in range(k):
        m = jnp.max(s, axis=0, keepdims=True)
        am = jnp.min(jnp.where(s == m, iota, n), axis=0, keepdims=True)
        vals.append(m)
        idxs.append(am)
        s = jnp.where(iota == am, -jnp.inf, s)
    return jnp.concatenate(vals, axis=0), jnp.concatenate(idxs, axis=0)


def _select_rows(sel, table):
    out = jnp.zeros(sel.shape, table.dtype)
    for r_ in range(table.shape[0]):
        out = jnp.where(sel == r_, table[r_:r_ + 1, :], out)
    return out


def _peer_route_kernel(h_ref, wq_ref, k1_ref, k2_ref, idx_ref, gate_ref, q_scr, *, tt):
    q_scr[...] = _dot(h_ref[...].astype(BF16), wq_ref[...]).astype(BF16)
    nsub = h_ref.shape[0] // tt

    def one(it, _):
        head = it % PEER_HEADS
        sub = it // PEER_HEADS
        roff = pl.multiple_of(sub * tt, tt)
        coff = pl.multiple_of(head * PEER_QDIM, PEER_QDIM)
        q1 = q_scr[pl.ds(roff, tt), pl.ds(coff, PEER_HALF)]
        q2 = q_scr[pl.ds(roff, tt), pl.ds(coff + PEER_HALF, PEER_HALF)]
        v1, i1 = _topk_rows(_dot_nt(k1_ref[...], q1), PEER_TOPK)
        v2, i2 = _topk_rows(_dot_nt(k2_ref[...], q2), PEER_TOPK)
        cand = jnp.concatenate([v1[i:i + 1, :] + v2 for i in range(PEER_TOPK)], axis=0)
        top_s, pos = _topk_rows(cand, PEER_TOPK)
        e1 = _select_rows(pos // PEER_TOPK, i1)
        e2 = _select_rows(pos % PEER_TOPK, i2)
        ex = jnp.exp(top_s - top_s[0:1, :])
        gates = ex / jnp.sum(ex, axis=0, keepdims=True)
        hoff = pl.multiple_of(head * PEER_TOPK, PEER_TOPK)
        idx_ref[pl.ds(hoff, PEER_TOPK), pl.ds(roff, tt)] = e1 * PEER_KEYS + e2
        gate_ref[pl.ds(hoff, PEER_TOPK), pl.ds(roff, tt)] = gates
        return 0

    lax.fori_loop(0, nsub * PEER_HEADS, one, 0)


def _peer_route(h1, wq, k1, k2, tm=256, tt=128):
    n, d = h1.shape
    const = lambda i: (0, 0)
    nsel = PEER_HEADS * PEER_TOPK
    return pl.pallas_call(
        functools.partial(_peer_route_kernel, tt=tt),
        grid=(n // tm,),
        in_specs=[
            pl.BlockSpec((tm, d), lambda i: (i, 0)),
            pl.BlockSpec(wq.shape, const),
            pl.BlockSpec(k1.shape, const),
            pl.BlockSpec(k2.shape, const),
        ],
        out_specs=[pl.BlockSpec((nsel, tm), lambda i: (0, i)), pl.BlockSpec((nsel, tm), lambda i: (0, i))],
        out_shape=[jax.ShapeDtypeStruct((nsel, n), I32), jax.ShapeDtypeStruct((nsel, n), F32)],
        scratch_shapes=[pltpu.VMEM((tm, PEER_HEADS * PEER_QDIM), BF16)],
        compiler_params=_cparams("parallel"),
        name="peer_route",
    )(h1, wq, k1, k2)


def _dot_split3(x, sel):
    hi = x.astype(BF16)
    r1 = x - hi.astype(F32)
    mid = r1.astype(BF16)
    lo = (r1 - mid.astype(F32)).astype(BF16)
    return _dot(hi, sel) + _dot(mid, sel) + _dot(lo, sel)


def _gelu_gate_kernel(part_ref, gate_ref, sel_ref, selt_ref, o_ref):
    a = _dot_split3(part_ref[...], sel_ref[...])
    w = gate_ref[...] * (0.5 * a * (1.0 + lax.erf(a * (2.0 ** -0.5))))
    o_ref[...] = _dot_split3(w, selt_ref[...])


def _gelu_gate(part, gates, tm=512):
    n, wide = part.shape
    nsel = gates.shape[1]
    lanes = wide // nsel
    sel = (jnp.arange(wide)[:, None] // lanes == jnp.arange(nsel)[None, :]).astype(BF16)
    const = lambda i: (0, 0)
    return pl.pallas_call(
        _gelu_gate_kernel,
        grid=(n // tm,),
        in_specs=[pl.BlockSpec((tm, wide), lambda i: (i, 0)), pl.BlockSpec((tm, nsel), lambda i: (i, 0)),
                  pl.BlockSpec((wide, nsel), const), pl.BlockSpec((nsel, wide), const)],
        out_specs=pl.BlockSpec((tm, wide), lambda i: (i, 0)),
        out_shape=jax.ShapeDtypeStruct((n, wide), F32),
        compiler_params=_cparams("parallel"),
        name="gelu_gate",
    )(part, gates, sel, sel.T)


def _final_kernel(h_ref, ffn_ref, p_ref, wg_ref, wp_ref, g_ref, b_ref, o_ref):
    h = h_ref[...]
    ple = _sigmoid(_dot(h.astype(BF16), wg_ref[...])) * _dot(p_ref[...].astype(BF16), wp_ref[...])
    o_ref[...] = _layer_norm(ALPHA * h + ffn_ref[...] + ple, g_ref[...], b_ref[...])


def _final(h1, ffn, p2, wg, wp, g, b, tm=512):
    n, d = h1.shape
    const = lambda i: (0, 0)
    rowblk = lambda w: pl.BlockSpec((tm, w), lambda i: (i, 0))
    vec = pl.BlockSpec((1, d), const)
    return pl.pallas_call(
        _final_kernel,
        grid=(n // tm,),
        in_specs=[rowblk(d), rowblk(d), rowblk(p2.shape[1]), pl.BlockSpec(wg.shape, const),
                  pl.BlockSpec(wp.shape, const), vec, vec],
        out_specs=rowblk(d),
        out_shape=jax.ShapeDtypeStruct((n, d), F32),
        compiler_params=_cparams("parallel"),
        name="ple_final_ln",
    )(h1, ffn, p2, wg, wp, g, b)


SC_CORES = 2
SC_SUBCORES = 16
SC_LANES = 16
SC_WORKERS = SC_CORES * SC_SUBCORES
PEER_NSEL = PEER_HEADS * PEER_TOPK
SC_ROWS = 16
SC_NBUF = 4
SC_GROUP = 8


def _sc_mesh():
    return plsc.VectorSubcoreMesh(core_axis_name="c", subcore_axis_name="s", num_cores=SC_CORES,
                                  num_subcores=SC_SUBCORES)


def _sc_token_pipeline(tab_hbm, idx_v, buf, sems, compute):
    nchunk = PEER_NSEL // SC_ROWS
    nsteps = SC_GROUP * nchunk
    assert nsteps % SC_NBUF == 0

    def gather(step, slot):
        t = step // nchunk
        c = step % nchunk
        return pltpu.make_async_copy(tab_hbm.at[idx_v.at[t, pl.ds(c * SC_ROWS, SC_ROWS)]], buf.at[slot],
                                     sems.at[slot])

    for b in range(SC_NBUF - 1):
        gather(b, b).start()

    def ring_round(q, _):
        for b in range(SC_NBUF):
            s = q * SC_NBUF + b
            ahead = s + SC_NBUF - 1

            @pl.when(ahead < nsteps)
            def _():
                gather(ahead, (b + SC_NBUF - 1) % SC_NBUF).start()

            gather(s, b).wait()
            compute(s // nchunk, s % nchunk, b)
        return 0

    lax.fori_loop(0, nsteps // SC_NBUF, ring_round, 0)


def _peer_act_sc(h1, idx, u_tab):
    n, d = h1.shape
    per_w = n // SC_WORKERS
    hold = 32

    @functools.partial(
        pl.kernel, mesh=_sc_mesh(),
        out_type=jax.ShapeDtypeStruct((n, PEER_NSEL * SC_LANES), F32),
        scratch_types=[
            pltpu.VMEM((SC_GROUP, d), F32),
            pltpu.VMEM((SC_GROUP, PEER_NSEL), I32),
            pltpu.VMEM((SC_GROUP, PEER_NSEL * SC_LANES), F32),
            pltpu.VMEM((SC_NBUF, SC_ROWS, d), F32),
            pltpu.SemaphoreType.DMA((SC_NBUF,)),
        ],
        name="peer_act_sc",
    )
    def k(h_hbm, idx_hbm, u_hbm, out_hbm, h_v, idx_v, out_v, buf, sems):
        base = (lax.axis_index("s") * SC_CORES + lax.axis_index("c")) * per_w

        def compute(t, c, slot):
            for jb in range(d // (hold * SC_LANES)):
                col0 = jb * hold * SC_LANES
                hv = [h_v[t, pl.ds(col0 + jj * SC_LANES, SC_LANES)] for jj in range(hold)]

                @plsc.parallel_loop(0, SC_ROWS)
                def _(r):
                    ps = [buf[slot, r, pl.ds(col0 + jj * SC_LANES, SC_LANES)] * hv[jj] for jj in range(hold)]
                    while len(ps) > 1:
                        ps = [ps[i] + ps[i + 1] for i in range(0, len(ps), 2)]
                    dst = out_v.at[t, pl.ds((c * SC_ROWS + r) * SC_LANES, SC_LANES)]
                    if jb == 0:
                        dst[...] = ps[0]
                    else:
                        plsc.addupdate(dst, ps[0])

        def group(g, _):
            tok0 = base + g * SC_GROUP
            pltpu.sync_copy(h_hbm.at[pl.ds(tok0, SC_GROUP)], h_v)
            pltpu.sync_copy(idx_hbm.at[pl.ds(tok0, SC_GROUP)], idx_v)
            _sc_token_pipeline(u_hbm, idx_v, buf, sems, compute)
            pltpu.sync_copy(out_v, out_hbm.at[pl.ds(tok0, SC_GROUP)])
            return 0

        lax.fori_loop(0, per_w // SC_GROUP, group, 0)

    return k(h1, idx, u_tab)


def _peer_combine_sc(wexp, idx, v_tab):
    n = idx.shape[0]
    d = v_tab.shape[1]
    per_w = n // SC_WORKERS
    nvec = d // SC_LANES
    ncol = 4

    @functools.partial(
        pl.kernel, mesh=_sc_mesh(),
        out_type=jax.ShapeDtypeStruct((n, d), F32),
        scratch_types=[
            pltpu.VMEM((SC_GROUP, PEER_NSEL * SC_LANES), F32),
            pltpu.VMEM((SC_GROUP, PEER_NSEL), I32),
            pltpu.VMEM((SC_GROUP, d), F32),
            pltpu.VMEM((SC_NBUF, SC_ROWS, d), F32),
            pltpu.SemaphoreType.DMA((SC_NBUF,)),
        ],
        name="peer_combine_sc",
    )
    def k(w_hbm, idx_hbm, v_hbm, out_hbm, w_v, idx_v, o_v, buf, sems):
        base = (lax.axis_index("s") * SC_CORES + lax.axis_index("c")) * per_w

        def compute(t, c, slot):
            @plsc.parallel_loop(0, nvec // ncol, unroll=2)
            def _(cb):
                col0 = cb * ncol * SC_LANES
                acc = [None] * ncol
                for r in range(SC_ROWS):
                    w = w_v[t, pl.ds((c * SC_ROWS + r) * SC_LANES, SC_LANES)]
                    for kk in range(ncol):
                        x = w * buf[slot, r, pl.ds(col0 + kk * SC_LANES, SC_LANES)]
                        acc[kk] = x if acc[kk] is None else acc[kk] + x
                for kk in range(ncol):
                    plsc.addupdate(o_v.at[t, pl.ds(col0 + kk * SC_LANES, SC_LANES)], acc[kk])

        def group(g, _):
            tok0 = base + g * SC_GROUP
            pltpu.sync_copy(w_hbm.at[pl.ds(tok0, SC_GROUP)], w_v)
            pltpu.sync_copy(idx_hbm.at[pl.ds(tok0, SC_GROUP)], idx_v)

            def zero(i, _):
                o_v[i // nvec, pl.ds((i % nvec) * SC_LANES, SC_LANES)] = jnp.zeros((SC_LANES,), F32)
                return 0

            lax.fori_loop(0, SC_GROUP * nvec, zero, 0)
            _sc_token_pipeline(v_hbm, idx_v, buf, sems, compute)
            pltpu.sync_copy(o_v, out_hbm.at[pl.ds(tok0, SC_GROUP)])
            return 0

        lax.fori_loop(0, per_w // SC_GROUP, group, 0)

    return k(wexp, idx, v_tab)


def _token_mixer_and_norm(x2, b, s, g0, b0, wa, wif, wg, gbias, conv_w, conv_b, wsb, wml, wout, g1, b1):
    n = x2.shape[0]
    n_if = 2 * ML_HEADS
    pa, gif, gate = _inproj(x2, g0, b0, wa, wif, wg)
    pa3 = pa.reshape(b, s, PA_WIDTH)
    o_sb = _sb_attention(pa3)
    qk3 = _conv_silu(pa3, conv_w, conv_b)
    grow = jnp.swapaxes(gif.reshape(b, s, LANES)[:, :, :n_if], 1, 2)
    h_ml = _mlstm(qk3, pa3, grow, gbias)
    return _merge(x2, o_sb.reshape(n, SB_WIDTH), h_ml.reshape(n, ML_WIDTH), pa, gate, g0, b0, wsb, wml, wout, g1, b1)


def kernel(x, p, ln0_g, ln0_b, w_in, b_igate, b_fgate, conv_w, conv_b, w_branch_sb, w_branch_ml, w_out, ln1_g, ln1_b, peer_wq, peer_k1, peer_k2, peer_u, peer_v, w_ple_gate, w_ple, ln2_g, ln2_b):
    b, s, d = x.shape
    assert w_in.shape[0] == DEPTH
    i = 0
    row = lambda v: v.reshape(1, -1)
    n_if = 2 * ML_HEADS
    wa = w_in[i][:, :PA_WIDTH].astype(BF16)
    wif = jnp.pad(w_in[i][:, PA_WIDTH:PA_WIDTH + n_if], ((0, 0), (0, LANES - n_if))).astype(BF16)
    wg = w_in[i][:, PA_WIDTH + n_if:].astype(BF16)
    gbias = jnp.concatenate([b_igate[i], b_fgate[i]]).reshape(n_if, 1)
    wsb, wml, wout = w_branch_sb[i].astype(BF16), w_branch_ml[i].astype(BF16), w_out[i].astype(BF16)
    wq, k1, k2 = peer_wq[i].astype(BF16), peer_k1[i].astype(BF16), peer_k2[i].astype(BF16)
    wpg, wp = w_ple_gate[i].astype(BF16), w_ple[i].astype(BF16)

    outs = []
    for bi in range(b):
        x2 = x[bi]
        h1 = _token_mixer_and_norm(x2, 1, s, row(ln0_g), row(ln0_b), wa, wif, wg, gbias, conv_w[i], row(conv_b[i]),
                                   wsb, wml, wout, row(ln1_g[i]), row(ln1_b[i]))
        idx_t, gate_t = _peer_route(h1, wq, k1, k2)
        idx = idx_t.T
        part = _peer_act_sc(h1, idx, peer_u[i])
        wexp = _gelu_gate(part, gate_t.T)
        ffn = _peer_combine_sc(wexp, idx, peer_v[i])
        outs.append(_final(h1, ffn, p[i, bi], wpg, wp, row(ln2_g[i]), row(ln2_b[i])))
    return jnp.stack(outs)
```

```python
import functools

import jax
import jax.numpy as jnp
from jax import lax
from jax.experimental import pallas as pl
from jax.experimental.pallas import tpu as pltpu
from jax.experimental.pallas import tpu_sc as plsc

F32 = jnp.float32
BF16 = jnp.bfloat16
I32 = jnp.int32

D_MODEL = 1024
SB_HEADS = 8
SB_HEAD_DIM = 64
SB_WIDTH = SB_HEADS * SB_HEAD_DIM
ML_HEADS = 4
ML_HEAD_DIM = 128
ML_WIDTH = ML_HEADS * ML_HEAD_DIM
ML_CHUNK = 128
CONV_WIDTH = 4
PEER_HEADS = 8
PEER_KEYS = 128
PEER_QDIM = 256
PEER_HALF = PEER_QDIM // 2
PEER_TOPK = 16
PEER_BLOCK = 128
DEPTH = 1
ALPHA = (2.0 * DEPTH) ** 0.25
LN_EPS = 1e-5

LANES = 128
VMEM_LIMIT = 56 * 1024 * 1024
SB_SKIP_LOG = -104.0

PA_WIDTH = 3 * SB_WIDTH + 4 * ML_WIDTH


def _cparams(*sem):
    return pltpu.CompilerParams(dimension_semantics=sem, vmem_limit_bytes=VMEM_LIMIT)


def _layer_norm(x, g, b):
    mu = jnp.mean(x, axis=-1, keepdims=True)
    xc = x - mu
    var = jnp.mean(xc * xc, axis=-1, keepdims=True)
    return xc * lax.rsqrt(var + LN_EPS) * g + b


def _log_sigmoid(z):
    return jnp.minimum(z, 0.0) - jnp.log1p(jnp.exp(-jnp.abs(z)))


def _sigmoid(z):
    return 1.0 / (1.0 + jnp.exp(-z))


def _dot(a, b):
    return jnp.dot(a, b, preferred_element_type=F32)


def _dot_nt(a, b):
    return lax.dot_general(a, b, (((1,), (1,)), ((), ())), preferred_element_type=F32)


def _dot_tn(a, b):
    return lax.dot_general(a, b, (((0,), (0,)), ((), ())), preferred_element_type=F32)


def _inproj_kernel(x_ref, g_ref, b_ref, wa_ref, wif_ref, wg_ref, oa_ref, oif_ref, og_ref, *, cw):
    h = _layer_norm(x_ref[...], g_ref[...], b_ref[...]).astype(BF16)
    for j in range(0, wa_ref.shape[1], cw):
        oa_ref[:, j:j + cw] = _dot(h, wa_ref[:, j:j + cw]).astype(BF16)
    oif_ref[...] = _dot(h, wif_ref[...])
    for j in range(0, wg_ref.shape[1], cw):
        og_ref[:, j:j + cw] = _dot(h, wg_ref[:, j:j + cw]).astype(BF16)


def _inproj(x3, bi, g, b, wa, wif, wg, tm=512):
    _, n, d = x3.shape
    const = lambda i: (0, 0)
    return pl.pallas_call(
        functools.partial(_inproj_kernel, cw=512),
        grid=(n // tm,),
        in_specs=[
            pl.BlockSpec((None, tm, d), lambda i: (bi, i, 0)),
            pl.BlockSpec((1, d), const),
            pl.BlockSpec((1, d), const),
            pl.BlockSpec(wa.shape, const),
            pl.BlockSpec(wif.shape, const),
            pl.BlockSpec(wg.shape, const),
        ],
        out_specs=[
            pl.BlockSpec((tm, wa.shape[1]), lambda i: (i, 0)),
            pl.BlockSpec((tm, wif.shape[1]), lambda i: (i, 0)),
            pl.BlockSpec((tm, wg.shape[1]), lambda i: (i, 0)),
        ],
        out_shape=[
            jax.ShapeDtypeStruct((n, wa.shape[1]), BF16),
            jax.ShapeDtypeStruct((n, wif.shape[1]), F32),
            jax.ShapeDtypeStruct((n, wg.shape[1]), BF16),
        ],
        compiler_params=_cparams("parallel"),
        name="ln_inproj",
    )(x3, g, b, wa, wif, wg)


def _sb_kernel(q_ref, k_ref, v_ref, o_ref, *, tq):
    qi = pl.program_id(2)
    q = q_ref[0]
    lane = lax.broadcasted_iota(I32, (1, LANES), 1)
    row = lax.broadcasted_iota(I32, (tq, tq), 0)
    col = lax.broadcasted_iota(I32, (tq, tq), 1)
    causal = col < row
    later = (row > col).astype(BF16)
    scale = SB_HEAD_DIM ** -0.5

    def tile(qh, kb, carry, masked):
        off = pl.multiple_of(kb * tq, tq)
        k_blk = k_ref[0, pl.ds(off, tq), :]
        v_blk = v_ref[0, pl.ds(off, tq), :]
        z = _dot_nt(qh, k_blk)
        lb = _log_sigmoid(z)
        lom = lb - z
        if masked:
            lom = jnp.where(causal, lom, 0.0)
        hi = lom.astype(BF16)
        lo = (lom - hi.astype(F32)).astype(BF16)
        tail = _dot(hi, later) + _dot(lo, later) + carry
        w = jnp.exp(lb + tail)
        if masked:
            w = jnp.where(causal, w, 0.0)
        contrib = _dot(w.astype(BF16), v_blk)
        return contrib, carry + jnp.sum(lom, axis=1, keepdims=True)

    acc_total = jnp.zeros((tq, LANES), F32)
    for hh in range(2):
        hmask = (lane >= SB_HEAD_DIM * hh) & (lane < SB_HEAD_DIM * (hh + 1))
        qh = (jnp.where(hmask, q, jnp.zeros_like(q)).astype(F32) * scale).astype(BF16)
        acc, carry = tile(qh, qi, jnp.zeros((tq, 1), F32), True)

        def cond(st):
            kb, _, _, cmax = st
            return jnp.logical_and(kb >= 0, cmax > SB_SKIP_LOG)

        def body(st):
            kb, acc, carry, _ = st
            contrib, carry = tile(qh, kb, carry, False)
            return kb - 1, acc + contrib, carry, jnp.max(carry)

        _, acc, _, _ = lax.while_loop(cond, body, (qi - 1, acc, carry, jnp.max(carry)))
        acc_total = jnp.where(hmask, acc, acc_total)
    o_ref[0] = acc_total.astype(o_ref.dtype)


def _sb_attention(pa3, tq=128):
    b, s, _ = pa3.shape
    npair = SB_WIDTH // LANES
    return pl.pallas_call(
        functools.partial(_sb_kernel, tq=tq),
        grid=(b, npair, s // tq),
        in_specs=[
            pl.BlockSpec((1, tq, LANES), lambda bi, hp, qi: (bi, qi, hp)),
            pl.BlockSpec((1, s, LANES), lambda bi, hp, qi: (bi, 0, npair + hp)),
            pl.BlockSpec((1, s, LANES), lambda bi, hp, qi: (bi, 0, 2 * npair + hp)),
        ],
        out_specs=pl.BlockSpec((1, tq, LANES), lambda bi, hp, qi: (bi, qi, hp)),
        out_shape=jax.ShapeDtypeStruct((b, s, SB_WIDTH), BF16),
        compiler_params=_cparams("parallel", "parallel", "arbitrary"),
        name="sb_attention",
    )(pa3, pa3, pa3)


CONV_HALO = 16


def _conv_kernel(x_ref, prev_ref, w_ref, b_ref, o_ref, buf_ref, *, ts):
    si = pl.program_id(1)
    is_k = pl.program_id(2)
    prev = prev_ref[0].astype(F32)
    buf_ref[0:CONV_HALO, :] = jnp.where(si == 0, 0.0, prev)
    buf_ref[CONV_HALO:, :] = x_ref[0].astype(F32)
    y = b_ref[...] + w_ref[0:1, :] * buf_ref[CONV_HALO:, :]
    for j in range(1, CONV_WIDTH):
        y = y + w_ref[j:j + 1, :] * buf_ref[CONV_HALO - j:CONV_HALO - j + ts, :]
    y = y * _sigmoid(y)
    y = y * jnp.where(is_k == 1, ML_HEAD_DIM ** -0.5, 1.0)
    o_ref[0] = y.astype(o_ref.dtype)


def _conv_silu(pa3, conv_w, conv_b, ts=1024):
    b, s, _ = pa3.shape
    ts = min(ts, s)
    cb = ML_WIDTH
    base = 3 * SB_WIDTH // cb
    hb = ts // CONV_HALO
    return pl.pallas_call(
        functools.partial(_conv_kernel, ts=ts),
        grid=(b, s // ts, 2),
        in_specs=[
            pl.BlockSpec((1, ts, cb), lambda bi, si, j: (bi, si, base + j)),
            pl.BlockSpec((1, CONV_HALO, cb), lambda bi, si, j: (bi, jnp.maximum(si * hb - 1, 0), base + j)),
            pl.BlockSpec((CONV_WIDTH, cb), lambda bi, si, j: (0, j)),
            pl.BlockSpec((1, cb), lambda bi, si, j: (0, j)),
        ],
        out_specs=pl.BlockSpec((1, ts, cb), lambda bi, si, j: (bi, si, j)),
        out_shape=jax.ShapeDtypeStruct((b, s, 2 * ML_WIDTH), BF16),
        scratch_shapes=[pltpu.VMEM((ts + CONV_HALO, cb), F32)],
        compiler_params=_cparams("parallel", "parallel", "parallel"),
        name="conv_silu",
    )(pa3, pa3, conv_w, conv_b)


def _mlstm_kernel(q_ref, k_ref, v_ref, g_ref, gb_ref, o_ref):
    L = ML_CHUNK
    head = pl.program_id(1)
    nc = q_ref.shape[1] // L
    r = lax.broadcasted_iota(I32, (L, L), 0)
    c = lax.broadcasted_iota(I32, (L, L), 1)
    eye = r == c
    sub = lax.broadcasted_iota(I32, (2 * ML_HEADS, L), 0)

    def to_col(row):
        return jnp.sum(jnp.where(eye, jnp.broadcast_to(row, (L, L)), 0.0), axis=1, keepdims=True)

    def chunk(ci, st):
        c_st, n_st, m_st = st
        off = pl.multiple_of(ci * L, L)
        q = q_ref[0, pl.ds(off, L), :]
        k = k_ref[0, pl.ds(off, L), :]
        v = v_ref[0, pl.ds(off, L), :]
        g = g_ref[0, :, pl.ds(off, L)] + gb_ref[...]
        li_row = jnp.sum(jnp.where(sub == head, g, 0.0), axis=0, keepdims=True)
        lf_row = _log_sigmoid(jnp.sum(jnp.where(sub == head + ML_HEADS, g, 0.0), axis=0, keepdims=True))
        lf_b = jnp.broadcast_to(lf_row, (L, L))
        bcum_col = jnp.sum(jnp.where(c <= r, lf_b, 0.0), axis=1, keepdims=True)
        lf_col = to_col(lf_row)
        li_col = to_col(li_row)
        bcum_row = jnp.sum(jnp.where(r <= c, jnp.broadcast_to(lf_col, (L, L)), 0.0), axis=0, keepdims=True)
        b_last = jnp.sum(lf_row, axis=1, keepdims=True)

        d_log = jnp.where(c <= r, bcum_col - bcum_row + li_row, -jnp.inf)
        inter = bcum_col + m_st
        m_t = jnp.maximum(inter, jnp.max(d_log, axis=1, keepdims=True))
        a_t = jnp.exp(inter - m_t)
        s_w = _dot_nt(q, k) * jnp.exp(d_log - m_t)
        num = a_t * _dot_nt(q, c_st.astype(BF16)) + _dot(s_w.astype(BF16), v)
        den = a_t * jnp.sum(q.astype(F32) * n_st, axis=1, keepdims=True) + jnp.sum(s_w, axis=1, keepdims=True)
        o_ref[0, pl.ds(off, L), :] = (num / jnp.maximum(jnp.abs(den), jnp.exp(-m_t))).astype(o_ref.dtype)

        w_end = b_last - bcum_col + li_col
        m_loc = jnp.max(w_end, axis=0, keepdims=True)
        e_end = jnp.exp(w_end - m_loc)
        c_loc = _dot_tn((e_end * v.astype(F32)).astype(BF16), k)
        n_loc = jnp.sum(e_end * k.astype(F32), axis=0, keepdims=True)
        m_new = jnp.maximum(b_last + m_st, m_loc)
        a = jnp.exp(b_last + m_st - m_new)
        gg = jnp.exp(m_loc - m_new)
        return a * c_st + gg * c_loc, a * n_st + gg * n_loc, m_new

    init = (jnp.zeros((ML_HEAD_DIM, ML_HEAD_DIM), F32), jnp.zeros((1, ML_HEAD_DIM), F32), jnp.zeros((1, 1), F32))
    lax.fori_loop(0, nc, chunk, init)


def _mlstm(qk3, pa3, grow, gbias):
    b, s, _ = pa3.shape
    vbase = (3 * SB_WIDTH + 2 * ML_WIDTH) // ML_HEAD_DIM
    return pl.pallas_call(
        _mlstm_kernel,
        grid=(b, ML_HEADS),
        in_specs=[
            pl.BlockSpec((1, s, ML_HEAD_DIM), lambda bi, h: (bi, 0, h)),
            pl.BlockSpec((1, s, ML_HEAD_DIM), lambda bi, h: (bi, 0, ML_HEADS + h)),
            pl.BlockSpec((1, s, ML_HEAD_DIM), lambda bi, h: (bi, 0, vbase + h)),
            pl.BlockSpec((1, 2 * ML_HEADS, s), lambda bi, h: (bi, 0, 0)),
            pl.BlockSpec((2 * ML_HEADS, 1), lambda bi, h: (0, 0)),
        ],
        out_specs=pl.BlockSpec((1, s, ML_HEAD_DIM), lambda bi, h: (bi, 0, h)),
        out_shape=jax.ShapeDtypeStruct((b, s, ML_WIDTH), BF16),
        compiler_params=_cparams("parallel", "parallel"),
        name="mlstm",
    )(qk3, qk3, pa3, grow, gbias)


def _merge_kernel(x_ref, osb_ref, hml_ref, mlo_ref, gate_ref, g0_ref, b0_ref, wsb_ref, wml_ref, wout_ref,
                  g1_ref, b1_ref, o_ref):
    h0 = _layer_norm(x_ref[...], g0_ref[...], b0_ref[...])
    o_ml = (_sigmoid(mlo_ref[...].astype(F32)) * hml_ref[...].astype(F32)).astype(BF16)
    y = _sigmoid(gate_ref[:, :D_MODEL].astype(F32)) * _dot(osb_ref[...], wsb_ref[...])
    y = y + _sigmoid(gate_ref[:, D_MODEL:].astype(F32)) * _dot(o_ml, wml_ref[...])
    mix = _dot(y.astype(BF16), wout_ref[...])
    o_ref[...] = _layer_norm(ALPHA * h0 + mix, g1_ref[...], b1_ref[...])


def _merge(x3, bi, osb, hml, pa, gate, g0, b0, wsb, wml, wout, g1, b1, tm=512):
    _, n, d = x3.shape
    const = lambda i: (0, 0)
    rowblk = lambda w: pl.BlockSpec((tm, w), lambda i: (i, 0))
    vec = pl.BlockSpec((1, d), const)
    mlo_blk = (3 * SB_WIDTH + 3 * ML_WIDTH) // ML_WIDTH
    return pl.pallas_call(
        _merge_kernel,
        grid=(n // tm,),
        in_specs=[
            pl.BlockSpec((None, tm, d), lambda i: (bi, i, 0)), rowblk(SB_WIDTH), rowblk(ML_WIDTH),
            pl.BlockSpec((tm, ML_WIDTH), lambda i: (i, mlo_blk)),
            rowblk(2 * d), vec, vec,
            pl.BlockSpec(wsb.shape, const), pl.BlockSpec(wml.shape, const), pl.BlockSpec(wout.shape, const),
            vec, vec,
        ],
        out_specs=rowblk(d),
        out_shape=jax.ShapeDtypeStruct((n, d), F32),
        compiler_params=_cparams("parallel"),
        name="merge_outproj_ln",
    )(x3, osb, hml, pa, gate, g0, b0, wsb, wml, wout, g1, b1)


def _topk_rows(s, k):
    n = s.shape[0]
    iota = lax.broadcasted_iota(I32, s.shape, 0)
    vals, idxs = [], []
    for _ in range(k):
        m = jnp.max(s, axis=0, keepdims=True)
        am = jnp.min(jnp.where(s == m, iota, n), axis=0, keepdims=True)
        vals.append(m)
        idxs.append(am)
        s = jnp.where(iota == am, -jnp.inf, s)
    return jnp.concatenate(vals, axis=0), jnp.concatenate(idxs, axis=0)


def _select_rows(sel, table):
    out = jnp.zeros(sel.shape, table.dtype)
    for r_ in range(table.shape[0]):
        out = jnp.where(sel == r_, table[r_:r_ + 1, :], out)
    return out


def _peer_route_kernel(h_ref, wq_ref, k1_ref, k2_ref, idx_ref, gate_ref, q_scr, *, tt):
    q_scr[...] = _dot(h_ref[...].astype(BF16), wq_ref[...]).astype(BF16)
    nsub = h_ref.shape[0] // tt

    def one(it, _):
        head = it % PEER_HEADS
        sub = it // PEER_HEADS
        roff = pl.multiple_of(sub * tt, tt)
        coff = pl.multiple_of(head * PEER_QDIM, PEER_QDIM)
        q1 = q_scr[pl.ds(roff, tt), pl.ds(coff, PEER_HALF)]
        q2 = q_scr[pl.ds(roff, tt), pl.ds(coff + PEER_HALF, PEER_HALF)]
        v1, i1 = _topk_rows(_dot_nt(k1_ref[...], q1), PEER_TOPK)
        v2, i2 = _topk_rows(_dot_nt(k2_ref[...], q2), PEER_TOPK)
        cand = jnp.concatenate([v1[i:i + 1, :] + v2 for i in range(PEER_TOPK)], axis=0)
        top_s, pos = _topk_rows(cand, PEER_TOPK)
        e1 = _select_rows(pos // PEER_TOPK, i1)
        e2 = _select_rows(pos % PEER_TOPK, i2)
        ex = jnp.exp(top_s - top_s[0:1, :])
        gates = ex / jnp.sum(ex, axis=0, keepdims=True)
        hoff = pl.multiple_of(head * PEER_TOPK, PEER_TOPK)
        idx_ref[pl.ds(hoff, PEER_TOPK), pl.ds(roff, tt)] = e1 * PEER_KEYS + e2
        gate_ref[pl.ds(hoff, PEER_TOPK), pl.ds(roff, tt)] = gates
        return 0

    lax.fori_loop(0, nsub * PEER_HEADS, one, 0)


def _peer_route(h1, tok0, n, wq, k1, k2, tm=256, tt=128):
    d = h1.shape[1]
    off = tok0 // tm
    const = lambda i: (0, 0)
    nsel = PEER_HEADS * PEER_TOPK
    return pl.pallas_call(
        functools.partial(_peer_route_kernel, tt=tt),
        grid=(n // tm,),
        in_specs=[
            pl.BlockSpec((tm, d), lambda i: (off + i, 0)),
            pl.BlockSpec(wq.shape, const),
            pl.BlockSpec(k1.shape, const),
            pl.BlockSpec(k2.shape, const),
        ],
        out_specs=[pl.BlockSpec((nsel, tm), lambda i: (0, i)), pl.BlockSpec((nsel, tm), lambda i: (0, i))],
        out_shape=[jax.ShapeDtypeStruct((nsel, n), I32), jax.ShapeDtypeStruct((nsel, n), F32)],
        scratch_shapes=[pltpu.VMEM((tm, PEER_HEADS * PEER_QDIM), BF16)],
        compiler_params=_cparams("parallel"),
        name="peer_route",
    )(h1, wq, k1, k2)


def _dot_split3(x, sel):
    hi = x.astype(BF16)
    r1 = x - hi.astype(F32)
    mid = r1.astype(BF16)
    lo = (r1 - mid.astype(F32)).astype(BF16)
    return _dot(hi, sel) + _dot(mid, sel) + _dot(lo, sel)


def _gelu_gate_kernel(part_ref, gate_ref, sel_ref, selt_ref, o_ref):
    a = _dot_split3(part_ref[...], sel_ref[...])
    w = gate_ref[...] * (0.5 * a * (1.0 + lax.erf(a * (2.0 ** -0.5))))
    o_ref[...] = _dot_split3(w, selt_ref[...])


def _gelu_gate(part, gates, tm=512):
    n, wide = part.shape
    nsel = gates.shape[1]
    lanes = wide // nsel
    sel = (jnp.arange(wide)[:, None] // lanes == jnp.arange(nsel)[None, :]).astype(BF16)
    const = lambda i: (0, 0)
    return pl.pallas_call(
        _gelu_gate_kernel,
        grid=(n // tm,),
        in_specs=[pl.BlockSpec((tm, wide), lambda i: (i, 0)), pl.BlockSpec((tm, nsel), lambda i: (i, 0)),
                  pl.BlockSpec((wide, nsel), const), pl.BlockSpec((nsel, wide), const)],
        out_specs=pl.BlockSpec((tm, wide), lambda i: (i, 0)),
        out_shape=jax.ShapeDtypeStruct((n, wide), F32),
        compiler_params=_cparams("parallel"),
        name="gelu_gate",
    )(part, gates, sel, sel.T)


def _final_kernel(h_ref, ffn_ref, p_ref, wg_ref, wp_ref, g_ref, b_ref, o_ref):
    h = h_ref[...]
    ple = _sigmoid(_dot(h.astype(BF16), wg_ref[...])) * _dot(p_ref[...].astype(BF16), wp_ref[...])
    o_ref[...] = _layer_norm(ALPHA * h + ffn_ref[...] + ple, g_ref[...], b_ref[...])


def _final(h1, tok0, ffn, p4, bi, wg, wp, g, b, tm=512):
    n, d = ffn.shape
    off = tok0 // tm
    const = lambda i: (0, 0)
    rowblk = lambda w: pl.BlockSpec((tm, w), lambda i: (i, 0))
    vec = pl.BlockSpec((1, d), const)
    return pl.pallas_call(
        _final_kernel,
        grid=(n // tm,),
        in_specs=[pl.BlockSpec((tm, d), lambda i: (off + i, 0)), rowblk(d),
                  pl.BlockSpec((None, None, tm, p4.shape[3]), lambda i: (0, bi, off + i, 0)),
                  pl.BlockSpec(wg.shape, const), pl.BlockSpec(wp.shape, const), vec, vec],
        out_specs=rowblk(d),
        out_shape=jax.ShapeDtypeStruct((n, d), F32),
        compiler_params=_cparams("parallel"),
        name="ple_final_ln",
    )(h1, ffn, p4, wg, wp, g, b)


SC_CORES = 2
SC_SUBCORES = 16
SC_LANES = 16
SC_WORKERS = SC_CORES * SC_SUBCORES
PEER_NSEL = PEER_HEADS * PEER_TOPK
SC_ROWS = 16
SC_NBUF = 4
SC_GROUP = 8


def _sc_mesh():
    return plsc.VectorSubcoreMesh(core_axis_name="c", subcore_axis_name="s", num_cores=SC_CORES,
                                  num_subcores=SC_SUBCORES)


def _sc_token_pipeline(tab_hbm, idx_v, buf, sems, compute):
    nchunk = PEER_NSEL // SC_ROWS
    nsteps = SC_GROUP * nchunk
    assert nsteps % SC_NBUF == 0

    def gather(step, slot):
        t = step // nchunk
        c = step % nchunk
        return pltpu.make_async_copy(tab_hbm.at[idx_v.at[t, pl.ds(c * SC_ROWS, SC_ROWS)]], buf.at[slot],
                                     sems.at[slot])

    for b in range(SC_NBUF - 1):
        gather(b, b).start()

    def ring_round(q, _):
        for b in range(SC_NBUF):
            s = q * SC_NBUF + b
            ahead = s + SC_NBUF - 1

            @pl.when(ahead < nsteps)
            def _():
                gather(ahead, (b + SC_NBUF - 1) % SC_NBUF).start()

            gather(s, b).wait()
            compute(s // nchunk, s % nchunk, b)
        return 0

    lax.fori_loop(0, nsteps // SC_NBUF, ring_round, 0)


def _peer_act_sc(h1, tok0, idx, u_tab):
    n = idx.shape[0]
    d = h1.shape[1]
    per_w = n // SC_WORKERS
    hold = 32

    @functools.partial(
        pl.kernel, mesh=_sc_mesh(),
        out_type=jax.ShapeDtypeStruct((n, PEER_NSEL * SC_LANES), F32),
        scratch_types=[
            pltpu.VMEM((SC_GROUP, d), F32),
            pltpu.VMEM((SC_GROUP, PEER_NSEL), I32),
            pltpu.VMEM((SC_GROUP, PEER_NSEL * SC_LANES), F32),
            pltpu.VMEM((SC_NBUF, SC_ROWS, d), F32),
            pltpu.SemaphoreType.DMA((SC_NBUF,)),
        ],
        name="peer_act_sc",
    )
    def k(h_hbm, idx_hbm, u_hbm, out_hbm, h_v, idx_v, out_v, buf, sems):
        base = (lax.axis_index("s") * SC_CORES + lax.axis_index("c")) * per_w

        def compute(t, c, slot):
            for jb in range(d // (hold * SC_LANES)):
                col0 = jb * hold * SC_LANES
                hv = [h_v[t, pl.ds(col0 + jj * SC_LANES, SC_LANES)] for jj in range(hold)]

                @plsc.parallel_loop(0, SC_ROWS)
                def _(r):
                    ps = [buf[slot, r, pl.ds(col0 + jj * SC_LANES, SC_LANES)] * hv[jj] for jj in range(hold)]
                    while len(ps) > 1:
                        ps = [ps[i] + ps[i + 1] for i in range(0, len(ps), 2)]
                    dst = out_v.at[t, pl.ds((c * SC_ROWS + r) * SC_LANES, SC_LANES)]
                    if jb == 0:
                        dst[...] = ps[0]
                    else:
                        plsc.addupdate(dst, ps[0])

        def group(g, _):
            row0 = base + g * SC_GROUP
            pltpu.sync_copy(h_hbm.at[pl.ds(tok0 + row0, SC_GROUP)], h_v)
            pltpu.sync_copy(idx_hbm.at[pl.ds(row0, SC_GROUP)], idx_v)
            _sc_token_pipeline(u_hbm, idx_v, buf, sems, compute)
            pltpu.sync_copy(out_v, out_hbm.at[pl.ds(row0, SC_GROUP)])
            return 0

        lax.fori_loop(0, per_w // SC_GROUP, group, 0)

    return k(h1, idx, u_tab)


def _peer_combine_sc(wexp, idx, v_tab):
    n = idx.shape[0]
    d = v_tab.shape[1]
    per_w = n // SC_WORKERS
    nvec = d // SC_LANES
    ncol = 4

    @functools.partial(
        pl.kernel, mesh=_sc_mesh(),
        out_type=jax.ShapeDtypeStruct((n, d), F32),
        scratch_types=[
            pltpu.VMEM((SC_GROUP, PEER_NSEL * SC_LANES), F32),
            pltpu.VMEM((SC_GROUP, PEER_NSEL), I32),
            pltpu.VMEM((SC_GROUP, d), F32),
            pltpu.VMEM((SC_NBUF, SC_ROWS, d), F32),
            pltpu.SemaphoreType.DMA((SC_NBUF,)),
        ],
        name="peer_combine_sc",
    )
    def k(w_hbm, idx_hbm, v_hbm, out_hbm, w_v, idx_v, o_v, buf, sems):
        base = (lax.axis_index("s") * SC_CORES + lax.axis_index("c")) * per_w

        def compute(t, c, slot):
            @plsc.parallel_loop(0, nvec // ncol, unroll=2)
            def _(cb):
                col0 = cb * ncol * SC_LANES
                acc = [None] * ncol
                for r in range(SC_ROWS):
                    w = w_v[t, pl.ds((c * SC_ROWS + r) * SC_LANES, SC_LANES)]
                    for kk in range(ncol):
                        x = w * buf[slot, r, pl.ds(col0 + kk * SC_LANES, SC_LANES)]
                        acc[kk] = x if acc[kk] is None else acc[kk] + x
                for kk in range(ncol):
                    plsc.addupdate(o_v.at[t, pl.ds(col0 + kk * SC_LANES, SC_LANES)], acc[kk])

        def group(g, _):
            tok0 = base + g * SC_GROUP
            pltpu.sync_copy(w_hbm.at[pl.ds(tok0, SC_GROUP)], w_v)
            pltpu.sync_copy(idx_hbm.at[pl.ds(tok0, SC_GROUP)], idx_v)

            def zero(i, _):
                o_v[i // nvec, pl.ds((i % nvec) * SC_LANES, SC_LANES)] = jnp.zeros((SC_LANES,), F32)
                return 0

            lax.fori_loop(0, SC_GROUP * nvec, zero, 0)
            _sc_token_pipeline(v_hbm, idx_v, buf, sems, compute)
            pltpu.sync_copy(o_v, out_hbm.at[pl.ds(tok0, SC_GROUP)])
            return 0

        lax.fori_loop(0, per_w // SC_GROUP, group, 0)

    return k(wexp, idx, v_tab)


def _token_mixer_and_norm(x3, bi, g0, b0, wa, wif, wg, gbias, conv_w, conv_b, wsb, wml, wout, g1, b1):
    s = x3.shape[1]
    n_if = 2 * ML_HEADS
    pa, gif, gate = _inproj(x3, bi, g0, b0, wa, wif, wg)
    pa3 = pa.reshape(1, s, PA_WIDTH)
    o_sb = _sb_attention(pa3)
    qk3 = _conv_silu(pa3, conv_w, conv_b)
    grow = jnp.swapaxes(gif.reshape(1, s, LANES)[:, :, :n_if], 1, 2)
    h_ml = _mlstm(qk3, pa3, grow, gbias)
    return _merge(x3, bi, o_sb.reshape(s, SB_WIDTH), h_ml.reshape(s, ML_WIDTH), pa, gate, g0, b0, wsb, wml, wout,
                  g1, b1)


def _after(value, prev):
    if prev is None:
        return value
    return lax.optimization_barrier((value, prev))[0]


def kernel(x, p, ln0_g, ln0_b, w_in, b_igate, b_fgate, conv_w, conv_b, w_branch_sb, w_branch_ml, w_out, ln1_g, ln1_b, peer_wq, peer_k1, peer_k2, peer_u, peer_v, w_ple_gate, w_ple, ln2_g, ln2_b):
    b, s, d = x.shape
    assert w_in.shape[0] == DEPTH
    i = 0
    row = lambda v: v.reshape(1, -1)
    n_if = 2 * ML_HEADS
    wa = w_in[i][:, :PA_WIDTH].astype(BF16)
    wif = jnp.pad(w_in[i][:, PA_WIDTH:PA_WIDTH + n_if], ((0, 0), (0, LANES - n_if))).astype(BF16)
    wg = w_in[i][:, PA_WIDTH + n_if:].astype(BF16)
    gbias = jnp.concatenate([b_igate[i], b_fgate[i]]).reshape(n_if, 1)
    wsb, wml, wout = w_branch_sb[i].astype(BF16), w_branch_ml[i].astype(BF16), w_out[i].astype(BF16)
    wq, k1, k2 = peer_wq[i].astype(BF16), peer_k1[i].astype(BF16), peer_k2[i].astype(BF16)
    wpg, wp = w_ple_gate[i].astype(BF16), w_ple[i].astype(BF16)

    halves = 2
    sp = s // halves
    h1s, routed, parts, outs = {}, {}, {}, {}
    last = None

    def mix(e):
        h1s[e] = _token_mixer_and_norm(x, e, _after(row(ln0_g), last), row(ln0_b), wa, wif, wg, gbias, conv_w[i],
                                       row(conv_b[i]), wsb, wml, wout, row(ln1_g[i]), row(ln1_b[i]))
        return h1s[e]

    def route(pc):
        e, hf = divmod(pc, halves)
        idx_t, gate_t = _peer_route(_after(h1s[e], last), hf * sp, sp, wq, k1, k2)
        idx = idx_t.T
        routed[pc] = (idx, gate_t.T)
        parts[pc] = _peer_act_sc(h1s[e], hf * sp, idx, peer_u[i])
        return idx_t

    def gelu(pc):
        e, hf = divmod(pc, halves)
        idx, gates = routed[pc]
        wexp = _gelu_gate(parts[pc], _after(gates, last))
        ffn = _peer_combine_sc(wexp, idx, peer_v[i])
        outs[pc] = _final(h1s[e], hf * sp, ffn, p, e, wpg, wp, row(ln2_g[i]), row(ln2_b[i]))
        return wexp

    order = [(mix, 0), (route, 0), (route, 1)]
    for e in range(1, b):
        order += [(mix, e), (gelu, 2 * e - 2), (route, 2 * e), (gelu, 2 * e - 1), (route, 2 * e + 1)]
    order += [(gelu, 2 * b - 2), (gelu, 2 * b - 1)]
    for stage, arg in order:
        last = stage(arg)
    return jnp.concatenate([outs[pc] for pc in range(halves * b)]).reshape(b, s, d)
```

```python
import functools

import jax
import jax.numpy as jnp
from jax import lax
from jax.experimental import pallas as pl
from jax.experimental.pallas import tpu as pltpu
from jax.experimental.pallas import tpu_sc as plsc

F32 = jnp.float32
BF16 = jnp.bfloat16
I32 = jnp.int32

D_MODEL = 1024
SB_HEADS = 8
SB_HEAD_DIM = 64
SB_WIDTH = SB_HEADS * SB_HEAD_DIM
ML_HEADS = 4
ML_HEAD_DIM = 128
ML_WIDTH = ML_HEADS * ML_HEAD_DIM
ML_CHUNK = 128
CONV_WIDTH = 4
PEER_HEADS = 8
PEER_KEYS = 128
PEER_QDIM = 256
PEER_HALF = PEER_QDIM // 2
PEER_TOPK = 16
PEER_BLOCK = 128
DEPTH = 1
ALPHA = (2.0 * DEPTH) ** 0.25
LN_EPS = 1e-5

LANES = 128
VMEM_LIMIT = 56 * 1024 * 1024
SB_SKIP_LOG = -104.0

PA_WIDTH = 3 * SB_WIDTH + 4 * ML_WIDTH


def _cparams(*sem):
    return pltpu.CompilerParams(dimension_semantics=sem, vmem_limit_bytes=VMEM_LIMIT)


def _layer_norm(x, g, b):
    mu = jnp.mean(x, axis=-1, keepdims=True)
    xc = x - mu
    var = jnp.mean(xc * xc, axis=-1, keepdims=True)
    return xc * lax.rsqrt(var + LN_EPS) * g + b


def _log_sigmoid(z):
    return jnp.minimum(z, 0.0) - jnp.log1p(jnp.exp(-jnp.abs(z)))


def _sigmoid(z):
    return 1.0 / (1.0 + jnp.exp(-z))


def _dot(a, b):
    return jnp.dot(a, b, preferred_element_type=F32)


def _dot_nt(a, b):
    return lax.dot_general(a, b, (((1,), (1,)), ((), ())), preferred_element_type=F32)


def _dot_tn(a, b):
    return lax.dot_general(a, b, (((0,), (0,)), ((), ())), preferred_element_type=F32)


def _inproj_kernel(x_ref, g_ref, b_ref, wa_ref, wif_ref, wg_ref, oa_ref, oif_ref, og_ref, *, cw):
    h = _layer_norm(x_ref[...], g_ref[...], b_ref[...]).astype(BF16)
    for j in range(0, wa_ref.shape[1], cw):
        oa_ref[:, j:j + cw] = _dot(h, wa_ref[:, j:j + cw]).astype(BF16)
    oif_ref[...] = _dot(h, wif_ref[...])
    for j in range(0, wg_ref.shape[1], cw):
        og_ref[:, j:j + cw] = _dot(h, wg_ref[:, j:j + cw]).astype(BF16)


def _inproj(x3, bi, g, b, wa, wif, wg, tm=512):
    _, n, d = x3.shape
    const = lambda i: (0, 0)
    return pl.pallas_call(
        functools.partial(_inproj_kernel, cw=512),
        grid=(n // tm,),
        in_specs=[
            pl.BlockSpec((None, tm, d), lambda i: (bi, i, 0)),
            pl.BlockSpec((1, d), const),
            pl.BlockSpec((1, d), const),
            pl.BlockSpec(wa.shape, const),
            pl.BlockSpec(wif.shape, const),
            pl.BlockSpec(wg.shape, const),
        ],
        out_specs=[
            pl.BlockSpec((tm, wa.shape[1]), lambda i: (i, 0)),
            pl.BlockSpec((tm, wif.shape[1]), lambda i: (i, 0)),
            pl.BlockSpec((tm, wg.shape[1]), lambda i: (i, 0)),
        ],
        out_shape=[
            jax.ShapeDtypeStruct((n, wa.shape[1]), BF16),
            jax.ShapeDtypeStruct((n, wif.shape[1]), F32),
            jax.ShapeDtypeStruct((n, wg.shape[1]), BF16),
        ],
        compiler_params=_cparams("parallel"),
        name="ln_inproj",
    )(x3, g, b, wa, wif, wg)


def _sb_kernel(q_ref, k_ref, v_ref, o_ref, *, tq):
    qi = pl.program_id(2)
    q = q_ref[0]
    lane = lax.broadcasted_iota(I32, (1, LANES), 1)
    row = lax.broadcasted_iota(I32, (tq, tq), 0)
    col = lax.broadcasted_iota(I32, (tq, tq), 1)
    causal = col < row
    later = (row > col).astype(BF16)
    scale = SB_HEAD_DIM ** -0.5

    def tile(qh, kb, carry, masked):
        off = pl.multiple_of(kb * tq, tq)
        k_blk = k_ref[0, pl.ds(off, tq), :]
        v_blk = v_ref[0, pl.ds(off, tq), :]
        z = _dot_nt(qh, k_blk)
        lb = _log_sigmoid(z)
        lom = lb - z
        if masked:
            lom = jnp.where(causal, lom, 0.0)
        hi = lom.astype(BF16)
        lo = (lom - hi.astype(F32)).astype(BF16)
        tail = _dot(hi, later) + _dot(lo, later) + carry
        w = jnp.exp(lb + tail)
        if masked:
            w = jnp.where(causal, w, 0.0)
        contrib = _dot(w.astype(BF16), v_blk)
        return contrib, carry + jnp.sum(lom, axis=1, keepdims=True)

    acc_total = jnp.zeros((tq, LANES), F32)
    for hh in range(2):
        hmask = (lane >= SB_HEAD_DIM * hh) & (lane < SB_HEAD_DIM * (hh + 1))
        qh = (jnp.where(hmask, q, jnp.zeros_like(q)).astype(F32) * scale).astype(BF16)
        acc, carry = tile(qh, qi, jnp.zeros((tq, 1), F32), True)

        def cond(st):
            kb, _, _, cmax = st
            return jnp.logical_and(kb >= 0, cmax > SB_SKIP_LOG)

        def body(st):
            kb, acc, carry, _ = st
            contrib, carry = tile(qh, kb, carry, False)
            return kb - 1, acc + contrib, carry, jnp.max(carry)

        _, acc, _, _ = lax.while_loop(cond, body, (qi - 1, acc, carry, jnp.max(carry)))
        acc_total = jnp.where(hmask, acc, acc_total)
    o_ref[0] = acc_total.astype(o_ref.dtype)


def _sb_attention(pa3, tq=128):
    b, s, _ = pa3.shape
    npair = SB_WIDTH // LANES
    return pl.pallas_call(
        functools.partial(_sb_kernel, tq=tq),
        grid=(b, npair, s // tq),
        in_specs=[
            pl.BlockSpec((1, tq, LANES), lambda bi, hp, qi: (bi, qi, hp)),
            pl.BlockSpec((1, s, LANES), lambda bi, hp, qi: (bi, 0, npair + hp)),
            pl.BlockSpec((1, s, LANES), lambda bi, hp, qi: (bi, 0, 2 * npair + hp)),
        ],
        out_specs=pl.BlockSpec((1, tq, LANES), lambda bi, hp, qi: (bi, qi, hp)),
        out_shape=jax.ShapeDtypeStruct((b, s, SB_WIDTH), BF16),
        compiler_params=_cparams("parallel", "parallel", "arbitrary"),
        name="sb_attention",
    )(pa3, pa3, pa3)


CONV_HALO = 16


def _conv_kernel(x_ref, prev_ref, w_ref, b_ref, o_ref, buf_ref, *, ts):
    si = pl.program_id(1)
    is_k = pl.program_id(2)
    prev = prev_ref[0].astype(F32)
    buf_ref[0:CONV_HALO, :] = jnp.where(si == 0, 0.0, prev)
    buf_ref[CONV_HALO:, :] = x_ref[0].astype(F32)
    y = b_ref[...] + w_ref[0:1, :] * buf_ref[CONV_HALO:, :]
    for j in range(1, CONV_WIDTH):
        y = y + w_ref[j:j + 1, :] * buf_ref[CONV_HALO - j:CONV_HALO - j + ts, :]
    y = y * _sigmoid(y)
    y = y * jnp.where(is_k == 1, ML_HEAD_DIM ** -0.5, 1.0)
    o_ref[0] = y.astype(o_ref.dtype)


def _conv_silu(pa3, conv_w, conv_b, ts=1024):
    b, s, _ = pa3.shape
    ts = min(ts, s)
    cb = ML_WIDTH
    base = 3 * SB_WIDTH // cb
    hb = ts // CONV_HALO
    return pl.pallas_call(
        functools.partial(_conv_kernel, ts=ts),
        grid=(b, s // ts, 2),
        in_specs=[
            pl.BlockSpec((1, ts, cb), lambda bi, si, j: (bi, si, base + j)),
            pl.BlockSpec((1, CONV_HALO, cb), lambda bi, si, j: (bi, jnp.maximum(si * hb - 1, 0), base + j)),
            pl.BlockSpec((CONV_WIDTH, cb), lambda bi, si, j: (0, j)),
            pl.BlockSpec((1, cb), lambda bi, si, j: (0, j)),
        ],
        out_specs=pl.BlockSpec((1, ts, cb), lambda bi, si, j: (bi, si, j)),
        out_shape=jax.ShapeDtypeStruct((b, s, 2 * ML_WIDTH), BF16),
        scratch_shapes=[pltpu.VMEM((ts + CONV_HALO, cb), F32)],
        compiler_params=_cparams("parallel", "parallel", "parallel"),
        name="conv_silu",
    )(pa3, pa3, conv_w, conv_b)


def _mlstm_kernel(q_ref, k_ref, v_ref, g_ref, gb_ref, o_ref):
    L = ML_CHUNK
    head = pl.program_id(1)
    nc = q_ref.shape[1] // L
    r = lax.broadcasted_iota(I32, (L, L), 0)
    c = lax.broadcasted_iota(I32, (L, L), 1)
    eye = r == c
    sub = lax.broadcasted_iota(I32, (2 * ML_HEADS, L), 0)

    def to_col(row):
        return jnp.sum(jnp.where(eye, jnp.broadcast_to(row, (L, L)), 0.0), axis=1, keepdims=True)

    def chunk(ci, st):
        c_st, n_st, m_st = st
        off = pl.multiple_of(ci * L, L)
        q = q_ref[0, pl.ds(off, L), :]
        k = k_ref[0, pl.ds(off, L), :]
        v = v_ref[0, pl.ds(off, L), :]
        g = g_ref[0, :, pl.ds(off, L)] + gb_ref[...]
        li_row = jnp.sum(jnp.where(sub == head, g, 0.0), axis=0, keepdims=True)
        lf_row = _log_sigmoid(jnp.sum(jnp.where(sub == head + ML_HEADS, g, 0.0), axis=0, keepdims=True))
        lf_b = jnp.broadcast_to(lf_row, (L, L))
        bcum_col = jnp.sum(jnp.where(c <= r, lf_b, 0.0), axis=1, keepdims=True)
        lf_col = to_col(lf_row)
        li_col = to_col(li_row)
        bcum_row = jnp.sum(jnp.where(r <= c, jnp.broadcast_to(lf_col, (L, L)), 0.0), axis=0, keepdims=True)
        b_last = jnp.sum(lf_row, axis=1, keepdims=True)

        d_log = jnp.where(c <= r, bcum_col - bcum_row + li_row, -jnp.inf)
        inter = bcum_col + m_st
        m_t = jnp.maximum(inter, jnp.max(d_log, axis=1, keepdims=True))
        a_t = jnp.exp(inter - m_t)
        s_w = _dot_nt(q, k) * jnp.exp(d_log - m_t)
        num = a_t * _dot_nt(q, c_st.astype(BF16)) + _dot(s_w.astype(BF16), v)
        den = a_t * jnp.sum(q.astype(F32) * n_st, axis=1, keepdims=True) + jnp.sum(s_w, axis=1, keepdims=True)
        o_ref[0, pl.ds(off, L), :] = (num / jnp.maximum(jnp.abs(den), jnp.exp(-m_t))).astype(o_ref.dtype)

        w_end = b_last - bcum_col + li_col
        m_loc = jnp.max(w_end, axis=0, keepdims=True)
        e_end = jnp.exp(w_end - m_loc)
        c_loc = _dot_tn((e_end * v.astype(F32)).astype(BF16), k)
        n_loc = jnp.sum(e_end * k.astype(F32), axis=0, keepdims=True)
        m_new = jnp.maximum(b_last + m_st, m_loc)
        a = jnp.exp(b_last + m_st - m_new)
        gg = jnp.exp(m_loc - m_new)
        return a * c_st + gg * c_loc, a * n_st + gg * n_loc, m_new

    init = (jnp.zeros((ML_HEAD_DIM, ML_HEAD_DIM), F32), jnp.zeros((1, ML_HEAD_DIM), F32), jnp.zeros((1, 1), F32))
    lax.fori_loop(0, nc, chunk, init)


def _mlstm(qk3, pa3, grow, gbias):
    b, s, _ = pa3.shape
    vbase = (3 * SB_WIDTH + 2 * ML_WIDTH) // ML_HEAD_DIM
    return pl.pallas_call(
        _mlstm_kernel,
        grid=(b, ML_HEADS),
        in_specs=[
            pl.BlockSpec((1, s, ML_HEAD_DIM), lambda bi, h: (bi, 0, h)),
            pl.BlockSpec((1, s, ML_HEAD_DIM), lambda bi, h: (bi, 0, ML_HEADS + h)),
            pl.BlockSpec((1, s, ML_HEAD_DIM), lambda bi, h: (bi, 0, vbase + h)),
            pl.BlockSpec((1, 2 * ML_HEADS, s), lambda bi, h: (bi, 0, 0)),
            pl.BlockSpec((2 * ML_HEADS, 1), lambda bi, h: (0, 0)),
        ],
        out_specs=pl.BlockSpec((1, s, ML_HEAD_DIM), lambda bi, h: (bi, 0, h)),
        out_shape=jax.ShapeDtypeStruct((b, s, ML_WIDTH), BF16),
        compiler_params=_cparams("parallel", "parallel"),
        name="mlstm",
    )(qk3, qk3, pa3, grow, gbias)


def _merge_kernel(x_ref, osb_ref, hml_ref, mlo_ref, gate_ref, g0_ref, b0_ref, wsb_ref, wml_ref, wout_ref,
                  g1_ref, b1_ref, o_ref):
    h0 = _layer_norm(x_ref[...], g0_ref[...], b0_ref[...])
    o_ml = (_sigmoid(mlo_ref[...].astype(F32)) * hml_ref[...].astype(F32)).astype(BF16)
    y = _sigmoid(gate_ref[:, :D_MODEL].astype(F32)) * _dot(osb_ref[...], wsb_ref[...])
    y = y + _sigmoid(gate_ref[:, D_MODEL:].astype(F32)) * _dot(o_ml, wml_ref[...])
    mix = _dot(y.astype(BF16), wout_ref[...])
    o_ref[...] = _layer_norm(ALPHA * h0 + mix, g1_ref[...], b1_ref[...])


def _merge(x3, bi, osb, hml, pa, gate, g0, b0, wsb, wml, wout, g1, b1, tm=512):
    _, n, d = x3.shape
    const = lambda i: (0, 0)
    rowblk = lambda w: pl.BlockSpec((tm, w), lambda i: (i, 0))
    vec = pl.BlockSpec((1, d), const)
    mlo_blk = (3 * SB_WIDTH + 3 * ML_WIDTH) // ML_WIDTH
    return pl.pallas_call(
        _merge_kernel,
        grid=(n // tm,),
        in_specs=[
            pl.BlockSpec((None, tm, d), lambda i: (bi, i, 0)), rowblk(SB_WIDTH), rowblk(ML_WIDTH),
            pl.BlockSpec((tm, ML_WIDTH), lambda i: (i, mlo_blk)),
            rowblk(2 * d), vec, vec,
            pl.BlockSpec(wsb.shape, const), pl.BlockSpec(wml.shape, const), pl.BlockSpec(wout.shape, const),
            vec, vec,
        ],
        out_specs=rowblk(d),
        out_shape=jax.ShapeDtypeStruct((n, d), F32),
        compiler_params=_cparams("parallel"),
        name="merge_outproj_ln",
    )(x3, osb, hml, pa, gate, g0, b0, wsb, wml, wout, g1, b1)


def _topk_rows(s, k):
    n = s.shape[0]
    iota = lax.broadcasted_iota(I32, s.shape, 0)
    vals, idxs = [], []
    for _ in range(k):
        m = jnp.max(s, axis=0, keepdims=True)
        am = jnp.min(jnp.where(s == m, iota, n), axis=0, keepdims=True)
        vals.append(m)
        idxs.append(am)
        s = jnp.where(iota == am, -jnp.inf, s)
    return jnp.concatenate(vals, axis=0), jnp.concatenate(idxs, axis=0)


def _select_rows(sel, table):
    out = jnp.zeros(sel.shape, table.dtype)
    for r_ in range(table.shape[0]):
        out = jnp.where(sel == r_, table[r_:r_ + 1, :], out)
    return out


def _peer_route_kernel(h_ref, wq_ref, k1_ref, k2_ref, idx_ref, gate_ref, q_scr, *, tt):
    q_scr[...] = _dot(h_ref[...].astype(BF16), wq_ref[...]).astype(BF16)
    nsub = h_ref.shape[0] // tt

    def one(it, _):
        head = it % PEER_HEADS
        sub = it // PEER_HEADS
        roff = pl.multiple_of(sub * tt, tt)
        coff = pl.multiple_of(head * PEER_QDIM, PEER_QDIM)
        q1 = q_scr[pl.ds(roff, tt), pl.ds(coff, PEER_HALF)]
        q2 = q_scr[pl.ds(roff, tt), pl.ds(coff + PEER_HALF, PEER_HALF)]
        v1, i1 = _topk_rows(_dot_nt(k1_ref[...], q1), PEER_TOPK)
        v2, i2 = _topk_rows(_dot_nt(k2_ref[...], q2), PEER_TOPK)
        cand = jnp.concatenate([v1[i:i + 1, :] + v2 for i in range(PEER_TOPK)], axis=0)
        top_s, pos = _topk_rows(cand, PEER_TOPK)
        e1 = _select_rows(pos // PEER_TOPK, i1)
        e2 = _select_rows(pos % PEER_TOPK, i2)
        ex = jnp.exp(top_s - top_s[0:1, :])
        gates = ex / jnp.sum(ex, axis=0, keepdims=True)
        hoff = pl.multiple_of(head * PEER_TOPK, PEER_TOPK)
        idx_ref[pl.ds(hoff, PEER_TOPK), pl.ds(roff, tt)] = e1 * PEER_KEYS + e2
        gate_ref[pl.ds(hoff, PEER_TOPK), pl.ds(roff, tt)] = gates
        return 0

    lax.fori_loop(0, nsub * PEER_HEADS, one, 0)


def _peer_route(h1, tok0, n, wq, k1, k2, tm=256, tt=128):
    d = h1.shape[1]
    off = tok0 // tm
    const = lambda i: (0, 0)
    nsel = PEER_HEADS * PEER_TOPK
    return pl.pallas_call(
        functools.partial(_peer_route_kernel, tt=tt),
        grid=(n // tm,),
        in_specs=[
            pl.BlockSpec((tm, d), lambda i: (off + i, 0)),
            pl.BlockSpec(wq.shape, const),
            pl.BlockSpec(k1.shape, const),
            pl.BlockSpec(k2.shape, const),
        ],
        out_specs=[pl.BlockSpec((nsel, tm), lambda i: (0, i)), pl.BlockSpec((nsel, tm), lambda i: (0, i))],
        out_shape=[jax.ShapeDtypeStruct((nsel, n), I32), jax.ShapeDtypeStruct((nsel, n), F32)],
        scratch_shapes=[pltpu.VMEM((tm, PEER_HEADS * PEER_QDIM), BF16)],
        compiler_params=_cparams("parallel"),
        name="peer_route",
    )(h1, wq, k1, k2)


def _dot_split3(x, sel):
    hi = x.astype(BF16)
    r1 = x - hi.astype(F32)
    mid = r1.astype(BF16)
    lo = (r1 - mid.astype(F32)).astype(BF16)
    return _dot(hi, sel) + _dot(mid, sel) + _dot(lo, sel)


def _gelu_gate_kernel(part_ref, gate_ref, sel_ref, selt_ref, o_ref):
    a = _dot_split3(part_ref[...], sel_ref[...])
    w = gate_ref[...] * (0.5 * a * (1.0 + lax.erf(a * (2.0 ** -0.5))))
    o_ref[...] = _dot_split3(w, selt_ref[...])


def _gelu_gate(part, gates, tm=512):
    n, wide = part.shape
    nsel = gates.shape[1]
    lanes = wide // nsel
    sel = (jnp.arange(wide)[:, None] // lanes == jnp.arange(nsel)[None, :]).astype(BF16)
    const = lambda i: (0, 0)
    return pl.pallas_call(
        _gelu_gate_kernel,
        grid=(n // tm,),
        in_specs=[pl.BlockSpec((tm, wide), lambda i: (i, 0)), pl.BlockSpec((tm, nsel), lambda i: (i, 0)),
                  pl.BlockSpec((wide, nsel), const), pl.BlockSpec((nsel, wide), const)],
        out_specs=pl.BlockSpec((tm, wide), lambda i: (i, 0)),
        out_shape=jax.ShapeDtypeStruct((n, wide), F32),
        compiler_params=_cparams("parallel"),
        name="gelu_gate",
    )(part, gates, sel, sel.T)


def _final_kernel(h_ref, ffn_ref, p_ref, wg_ref, wp_ref, g_ref, b_ref, o_ref):
    h = h_ref[...]
    ple = _sigmoid(_dot(h.astype(BF16), wg_ref[...])) * _dot(p_ref[...].astype(BF16), wp_ref[...])
    o_ref[...] = _layer_norm(ALPHA * h + ffn_ref[...] + ple, g_ref[...], b_ref[...])


def _final(h1, tok0, ffn, p4, bi, wg, wp, g, b, tm=512):
    n, d = ffn.shape
    off = tok0 // tm
    const = lambda i: (0, 0)
    rowblk = lambda w: pl.BlockSpec((tm, w), lambda i: (i, 0))
    vec = pl.BlockSpec((1, d), const)
    return pl.pallas_call(
        _final_kernel,
        grid=(n // tm,),
        in_specs=[pl.BlockSpec((tm, d), lambda i: (off + i, 0)), rowblk(d),
                  pl.BlockSpec((None, None, tm, p4.shape[3]), lambda i: (0, bi, off + i, 0)),
                  pl.BlockSpec(wg.shape, const), pl.BlockSpec(wp.shape, const), vec, vec],
        out_specs=rowblk(d),
        out_shape=jax.ShapeDtypeStruct((n, d), F32),
        compiler_params=_cparams("parallel"),
        name="ple_final_ln",
    )(h1, ffn, p4, wg, wp, g, b)


SC_CORES = 2
SC_SUBCORES = 16
SC_LANES = 16
SC_WORKERS = SC_CORES * SC_SUBCORES
PEER_NSEL = PEER_HEADS * PEER_TOPK
SC_ROWS = 16
SC_NBUF = 4
SC_GROUP = 8


def _sc_mesh():
    return plsc.VectorSubcoreMesh(core_axis_name="c", subcore_axis_name="s", num_cores=SC_CORES,
                                  num_subcores=SC_SUBCORES)


def _sc_token_pipeline(tab_hbm, idx_v, buf, sems, compute):
    nchunk = PEER_NSEL // SC_ROWS
    nsteps = SC_GROUP * nchunk
    assert nsteps % SC_NBUF == 0

    def gather(step, slot):
        t = step // nchunk
        c = step % nchunk
        return pltpu.make_async_copy(tab_hbm.at[idx_v.at[t, pl.ds(c * SC_ROWS, SC_ROWS)]], buf.at[slot],
                                     sems.at[slot])

    for b in range(SC_NBUF - 1):
        gather(b, b).start()

    def ring_round(q, _):
        for b in range(SC_NBUF):
            s = q * SC_NBUF + b
            ahead = s + SC_NBUF - 1

            @pl.when(ahead < nsteps)
            def _():
                gather(ahead, (b + SC_NBUF - 1) % SC_NBUF).start()

            gather(s, b).wait()
            compute(s // nchunk, s % nchunk, b)
        return 0

    lax.fori_loop(0, nsteps // SC_NBUF, ring_round, 0)


def _peer_act_sc(h1, tok0, idx, u_tab):
    n = idx.shape[0]
    d = h1.shape[1]
    per_w = n // SC_WORKERS
    hold = 32

    @functools.partial(
        pl.kernel, mesh=_sc_mesh(),
        out_type=jax.ShapeDtypeStruct((n, PEER_NSEL * SC_LANES), F32),
        scratch_types=[
            pltpu.VMEM((SC_GROUP, d), F32),
            pltpu.VMEM((SC_GROUP, PEER_NSEL), I32),
            pltpu.VMEM((SC_GROUP, PEER_NSEL * SC_LANES), F32),
            pltpu.VMEM((SC_NBUF, SC_ROWS, d), F32),
            pltpu.SemaphoreType.DMA((SC_NBUF,)),
        ],
        name="peer_act_sc",
    )
    def k(h_hbm, idx_hbm, u_hbm, out_hbm, h_v, idx_v, out_v, buf, sems):
        base = (lax.axis_index("s") * SC_CORES + lax.axis_index("c")) * per_w

        def compute(t, c, slot):
            for jb in range(d // (hold * SC_LANES)):
                col0 = jb * hold * SC_LANES
                hv = [h_v[t, pl.ds(col0 + jj * SC_LANES, SC_LANES)] for jj in range(hold)]

                @plsc.parallel_loop(0, SC_ROWS)
                def _(r):
                    ps = [buf[slot, r, pl.ds(col0 + jj * SC_LANES, SC_LANES)] * hv[jj] for jj in range(hold)]
                    while len(ps) > 1:
                        ps = [ps[i] + ps[i + 1] for i in range(0, len(ps), 2)]
                    dst = out_v.at[t, pl.ds((c * SC_ROWS + r) * SC_LANES, SC_LANES)]
                    if jb == 0:
                        dst[...] = ps[0]
                    else:
                        plsc.addupdate(dst, ps[0])

        def group(g, _):
            row0 = base + g * SC_GROUP
            pltpu.sync_copy(h_hbm.at[pl.ds(tok0 + row0, SC_GROUP)], h_v)
            pltpu.sync_copy(idx_hbm.at[pl.ds(row0, SC_GROUP)], idx_v)
            _sc_token_pipeline(u_hbm, idx_v, buf, sems, compute)
            pltpu.sync_copy(out_v, out_hbm.at[pl.ds(row0, SC_GROUP)])
            return 0

        lax.fori_loop(0, per_w // SC_GROUP, group, 0)

    return k(h1, idx, u_tab)


def _peer_combine_sc(wexp, idx, v_tab):
    n = idx.shape[0]
    d = v_tab.shape[1]
    per_w = n // SC_WORKERS
    nvec = d // SC_LANES
    ncol = 4

    @functools.partial(
        pl.kernel, mesh=_sc_mesh(),
        out_type=jax.ShapeDtypeStruct((n, d), F32),
        scratch_types=[
            pltpu.VMEM((SC_GROUP, PEER_NSEL * SC_LANES), F32),
            pltpu.VMEM((SC_GROUP, PEER_NSEL), I32),
            pltpu.VMEM((SC_GROUP, d), F32),
            pltpu.VMEM((SC_NBUF, SC_ROWS, d), F32),
            pltpu.SemaphoreType.DMA((SC_NBUF,)),
        ],
        name="peer_combine_sc",
    )
    def k(w_hbm, idx_hbm, v_hbm, out_hbm, w_v, idx_v, o_v, buf, sems):
        base = (lax.axis_index("s") * SC_CORES + lax.axis_index("c")) * per_w

        def compute(t, c, slot):
            @plsc.parallel_loop(0, nvec // ncol, unroll=2)
            def _(cb):
                col0 = cb * ncol * SC_LANES
                acc = [None] * ncol
                for r in range(SC_ROWS):
                    w = w_v[t, pl.ds((c * SC_ROWS + r) * SC_LANES, SC_LANES)]
                    for kk in range(ncol):
                        x = w * buf[slot, r, pl.ds(col0 + kk * SC_LANES, SC_LANES)]
                        acc[kk] = x if acc[kk] is None else acc[kk] + x
                for kk in range(ncol):
                    plsc.addupdate(o_v.at[t, pl.ds(col0 + kk * SC_LANES, SC_LANES)], acc[kk])

        def group(g, _):
            tok0 = base + g * SC_GROUP
            pltpu.sync_copy(w_hbm.at[pl.ds(tok0, SC_GROUP)], w_v)
            pltpu.sync_copy(idx_hbm.at[pl.ds(tok0, SC_GROUP)], idx_v)

            def zero(i, _):
                o_v[i // nvec, pl.ds((i % nvec) * SC_LANES, SC_LANES)] = jnp.zeros((SC_LANES,), F32)
                return 0

            lax.fori_loop(0, SC_GROUP * nvec, zero, 0)
            _sc_token_pipeline(v_hbm, idx_v, buf, sems, compute)
            pltpu.sync_copy(o_v, out_hbm.at[pl.ds(tok0, SC_GROUP)])
            return 0

        lax.fori_loop(0, per_w // SC_GROUP, group, 0)

    return k(wexp, idx, v_tab)


def _token_mixer_and_norm(x3, bi, g0, b0, wa, wif, wg, gbias, conv_w, conv_b, wsb, wml, wout, g1, b1):
    s = x3.shape[1]
    n_if = 2 * ML_HEADS
    pa, gif, gate = _inproj(x3, bi, g0, b0, wa, wif, wg)
    pa3 = pa.reshape(1, s, PA_WIDTH)
    o_sb = _sb_attention(pa3)
    qk3 = _conv_silu(pa3, conv_w, conv_b)
    grow = jnp.swapaxes(gif.reshape(1, s, LANES)[:, :, :n_if], 1, 2)
    h_ml = _mlstm(qk3, pa3, grow, gbias)
    return _merge(x3, bi, o_sb.reshape(s, SB_WIDTH), h_ml.reshape(s, ML_WIDTH), pa, gate, g0, b0, wsb, wml, wout,
                  g1, b1)


def _after(value, prev):
    if prev is None:
        return value
    return lax.optimization_barrier((value, prev))[0]


def kernel(x, p, ln0_g, ln0_b, w_in, b_igate, b_fgate, conv_w, conv_b, w_branch_sb, w_branch_ml, w_out, ln1_g, ln1_b, peer_wq, peer_k1, peer_k2, peer_u, peer_v, w_ple_gate, w_ple, ln2_g, ln2_b):
    b, s, d = x.shape
    assert w_in.shape[0] == DEPTH
    i = 0
    row = lambda v: v.reshape(1, -1)
    n_if = 2 * ML_HEADS
    wa = w_in[i][:, :PA_WIDTH].astype(BF16)
    wif = jnp.pad(w_in[i][:, PA_WIDTH:PA_WIDTH + n_if], ((0, 0), (0, LANES - n_if))).astype(BF16)
    wg = w_in[i][:, PA_WIDTH + n_if:].astype(BF16)
    gbias = jnp.concatenate([b_igate[i], b_fgate[i]]).reshape(n_if, 1)
    wsb, wml, wout = w_branch_sb[i].astype(BF16), w_branch_ml[i].astype(BF16), w_out[i].astype(BF16)
    wq, k1, k2 = peer_wq[i].astype(BF16), peer_k1[i].astype(BF16), peer_k2[i].astype(BF16)
    wpg, wp = w_ple_gate[i].astype(BF16), w_ple[i].astype(BF16)

    halves = 2
    sp = s // halves
    h1s, routed, parts, ffns, outs = {}, {}, {}, {}, {}
    last = None

    def mix(e):
        h1s[e] = _token_mixer_and_norm(x, e, _after(row(ln0_g), last), row(ln0_b), wa, wif, wg, gbias, conv_w[i],
                                       row(conv_b[i]), wsb, wml, wout, row(ln1_g[i]), row(ln1_b[i]))
        return h1s[e]

    def route(pc):
        e, hf = divmod(pc, halves)
        idx_t, gate_t = _peer_route(_after(h1s[e], last), hf * sp, sp, wq, k1, k2)
        idx = idx_t.T
        routed[pc] = (idx, gate_t.T)
        parts[pc] = _peer_act_sc(h1s[e], hf * sp, _after(idx, ffns.get(pc - 2)), peer_u[i])
        return idx_t

    def gelu(pc):
        e, hf = divmod(pc, halves)
        idx, gates = routed[pc]
        wexp = _gelu_gate(parts[pc], _after(gates, last))
        ffns[pc] = _peer_combine_sc(wexp, idx, peer_v[i])
        outs[pc] = _final(h1s[e], hf * sp, ffns[pc], p, e, wpg, wp, row(ln2_g[i]), row(ln2_b[i]))
        return wexp

    order = [(mix, 0), (route, 0), (route, 1)]
    for e in range(1, b):
        order += [(mix, e), (gelu, 2 * e - 2), (route, 2 * e), (gelu, 2 * e - 1), (route, 2 * e + 1)]
    order += [(gelu, 2 * b - 2), (gelu, 2 * b - 1)]
    for stage, arg in order:
        last = stage(arg)
    return jnp.concatenate([outs[pc] for pc in range(halves * b)]).reshape(b, s, d)
```

```python
import functools

import jax
import jax.numpy as jnp
from jax import lax
from jax.experimental import pallas as pl
from jax.experimental.pallas import tpu as pltpu
from jax.experimental.pallas import tpu_sc as plsc

F32 = jnp.float32
BF16 = jnp.bfloat16
I32 = jnp.int32

D_MODEL = 1024
SB_HEADS = 8
SB_HEAD_DIM = 64
SB_WIDTH = SB_HEADS * SB_HEAD_DIM
ML_HEADS = 4
ML_HEAD_DIM = 128
ML_WIDTH = ML_HEADS * ML_HEAD_DIM
ML_CHUNK = 128
CONV_WIDTH = 4
PEER_HEADS = 8
PEER_KEYS = 128
PEER_QDIM = 256
PEER_HALF = PEER_QDIM // 2
PEER_TOPK = 16
PEER_BLOCK = 128
DEPTH = 1
ALPHA = (2.0 * DEPTH) ** 0.25
LN_EPS = 1e-5

LANES = 128
VMEM_LIMIT = 56 * 1024 * 1024
SB_SKIP_LOG = -104.0

PA_WIDTH = 3 * SB_WIDTH + 4 * ML_WIDTH


def _cparams(*sem):
    return pltpu.CompilerParams(dimension_semantics=sem, vmem_limit_bytes=VMEM_LIMIT)


def _layer_norm(x, g, b):
    mu = jnp.mean(x, axis=-1, keepdims=True)
    xc = x - mu
    var = jnp.mean(xc * xc, axis=-1, keepdims=True)
    return xc * lax.rsqrt(var + LN_EPS) * g + b


def _log_sigmoid(z):
    return jnp.minimum(z, 0.0) - jnp.log1p(jnp.exp(-jnp.abs(z)))


def _sigmoid(z):
    return 1.0 / (1.0 + jnp.exp(-z))


def _dot(a, b):
    return jnp.dot(a, b, preferred_element_type=F32)


def _dot_nt(a, b):
    return lax.dot_general(a, b, (((1,), (1,)), ((), ())), preferred_element_type=F32)


def _dot_tn(a, b):
    return lax.dot_general(a, b, (((0,), (0,)), ((), ())), preferred_element_type=F32)


def _inproj_kernel(x_ref, g_ref, b_ref, wa_ref, wif_ref, wg_ref, oa_ref, oif_ref, og_ref, *, cw):
    h = _layer_norm(x_ref[...], g_ref[...], b_ref[...]).astype(BF16)
    for j in range(0, wa_ref.shape[1], cw):
        oa_ref[:, j:j + cw] = _dot(h, wa_ref[:, j:j + cw]).astype(BF16)
    oif_ref[...] = _dot(h, wif_ref[...])
    for j in range(0, wg_ref.shape[1], cw):
        og_ref[:, j:j + cw] = _dot(h, wg_ref[:, j:j + cw]).astype(BF16)


def _inproj(x3, bi, g, b, wa, wif, wg, tm=512):
    _, n, d = x3.shape
    const = lambda i: (0, 0)
    return pl.pallas_call(
        functools.partial(_inproj_kernel, cw=512),
        grid=(n // tm,),
        in_specs=[
            pl.BlockSpec((None, tm, d), lambda i: (bi, i, 0)),
            pl.BlockSpec((1, d), const),
            pl.BlockSpec((1, d), const),
            pl.BlockSpec(wa.shape, const),
            pl.BlockSpec(wif.shape, const),
            pl.BlockSpec(wg.shape, const),
        ],
        out_specs=[
            pl.BlockSpec((tm, wa.shape[1]), lambda i: (i, 0)),
            pl.BlockSpec((tm, wif.shape[1]), lambda i: (i, 0)),
            pl.BlockSpec((tm, wg.shape[1]), lambda i: (i, 0)),
        ],
        out_shape=[
            jax.ShapeDtypeStruct((n, wa.shape[1]), BF16),
            jax.ShapeDtypeStruct((n, wif.shape[1]), F32),
            jax.ShapeDtypeStruct((n, wg.shape[1]), BF16),
        ],
        compiler_params=_cparams("parallel"),
        name="ln_inproj",
    )(x3, g, b, wa, wif, wg)


def _sb_kernel(q_ref, k_ref, v_ref, o_ref, *, tq):
    qi = pl.program_id(2)
    q = q_ref[0]
    lane = lax.broadcasted_iota(I32, (1, LANES), 1)
    row = lax.broadcasted_iota(I32, (tq, tq), 0)
    col = lax.broadcasted_iota(I32, (tq, tq), 1)
    causal = col < row
    later = (row > col).astype(BF16)
    scale = SB_HEAD_DIM ** -0.5

    def tile(qh, kb, carry, masked):
        off = pl.multiple_of(kb * tq, tq)
        k_blk = k_ref[0, pl.ds(off, tq), :]
        v_blk = v_ref[0, pl.ds(off, tq), :]
        z = _dot_nt(qh, k_blk)
        lb = _log_sigmoid(z)
        lom = lb - z
        if masked:
            lom = jnp.where(causal, lom, 0.0)
        hi = lom.astype(BF16)
        lo = (lom - hi.astype(F32)).astype(BF16)
        tail = _dot(hi, later) + _dot(lo, later) + carry
        w = jnp.exp(lb + tail)
        if masked:
            w = jnp.where(causal, w, 0.0)
        contrib = _dot(w.astype(BF16), v_blk)
        return contrib, carry + jnp.sum(lom, axis=1, keepdims=True)

    acc_total = jnp.zeros((tq, LANES), F32)
    for hh in range(2):
        hmask = (lane >= SB_HEAD_DIM * hh) & (lane < SB_HEAD_DIM * (hh + 1))
        qh = (jnp.where(hmask, q, jnp.zeros_like(q)).astype(F32) * scale).astype(BF16)
        acc, carry = tile(qh, qi, jnp.zeros((tq, 1), F32), True)

        def cond(st):
            kb, _, _, cmax = st
            return jnp.logical_and(kb >= 0, cmax > SB_SKIP_LOG)

        def body(st):
            kb, acc, carry, _ = st
            contrib, carry = tile(qh, kb, carry, False)
            return kb - 1, acc + contrib, carry, jnp.max(carry)

        _, acc, _, _ = lax.while_loop(cond, body, (qi - 1, acc, carry, jnp.max(carry)))
        acc_total = jnp.where(hmask, acc, acc_total)
    o_ref[0] = acc_total.astype(o_ref.dtype)


def _sb_attention(pa3, tq=128):
    b, s, _ = pa3.shape
    npair = SB_WIDTH // LANES
    return pl.pallas_call(
        functools.partial(_sb_kernel, tq=tq),
        grid=(b, npair, s // tq),
        in_specs=[
            pl.BlockSpec((1, tq, LANES), lambda bi, hp, qi: (bi, qi, hp)),
            pl.BlockSpec((1, s, LANES), lambda bi, hp, qi: (bi, 0, npair + hp)),
            pl.BlockSpec((1, s, LANES), lambda bi, hp, qi: (bi, 0, 2 * npair + hp)),
        ],
        out_specs=pl.BlockSpec((1, tq, LANES), lambda bi, hp, qi: (bi, qi, hp)),
        out_shape=jax.ShapeDtypeStruct((b, s, SB_WIDTH), BF16),
        compiler_params=_cparams("parallel", "parallel", "arbitrary"),
        name="sb_attention",
    )(pa3, pa3, pa3)


CONV_HALO = 16


def _conv_kernel(x_ref, prev_ref, w_ref, b_ref, o_ref, buf_ref, *, ts):
    si = pl.program_id(1)
    is_k = pl.program_id(2)
    prev = prev_ref[0].astype(F32)
    buf_ref[0:CONV_HALO, :] = jnp.where(si == 0, 0.0, prev)
    buf_ref[CONV_HALO:, :] = x_ref[0].astype(F32)
    y = b_ref[...] + w_ref[0:1, :] * buf_ref[CONV_HALO:, :]
    for j in range(1, CONV_WIDTH):
        y = y + w_ref[j:j + 1, :] * buf_ref[CONV_HALO - j:CONV_HALO - j + ts, :]
    y = y * _sigmoid(y)
    y = y * jnp.where(is_k == 1, ML_HEAD_DIM ** -0.5, 1.0)
    o_ref[0] = y.astype(o_ref.dtype)


def _conv_silu(pa3, conv_w, conv_b, ts=1024):
    b, s, _ = pa3.shape
    ts = min(ts, s)
    cb = ML_WIDTH
    base = 3 * SB_WIDTH // cb
    hb = ts // CONV_HALO
    return pl.pallas_call(
        functools.partial(_conv_kernel, ts=ts),
        grid=(b, s // ts, 2),
        in_specs=[
            pl.BlockSpec((1, ts, cb), lambda bi, si, j: (bi, si, base + j)),
            pl.BlockSpec((1, CONV_HALO, cb), lambda bi, si, j: (bi, jnp.maximum(si * hb - 1, 0), base + j)),
            pl.BlockSpec((CONV_WIDTH, cb), lambda bi, si, j: (0, j)),
            pl.BlockSpec((1, cb), lambda bi, si, j: (0, j)),
        ],
        out_specs=pl.BlockSpec((1, ts, cb), lambda bi, si, j: (bi, si, j)),
        out_shape=jax.ShapeDtypeStruct((b, s, 2 * ML_WIDTH), BF16),
        scratch_shapes=[pltpu.VMEM((ts + CONV_HALO, cb), F32)],
        compiler_params=_cparams("parallel", "parallel", "parallel"),
        name="conv_silu",
    )(pa3, pa3, conv_w, conv_b)


def _mlstm_kernel(q_ref, k_ref, v_ref, g_ref, gb_ref, o_ref):
    L = ML_CHUNK
    head = pl.program_id(1)
    nc = q_ref.shape[1] // L
    r = lax.broadcasted_iota(I32, (L, L), 0)
    c = lax.broadcasted_iota(I32, (L, L), 1)
    eye = r == c
    sub = lax.broadcasted_iota(I32, (2 * ML_HEADS, L), 0)

    def to_col(row):
        return jnp.sum(jnp.where(eye, jnp.broadcast_to(row, (L, L)), 0.0), axis=1, keepdims=True)

    def chunk(ci, st):
        c_st, n_st, m_st = st
        off = pl.multiple_of(ci * L, L)
        q = q_ref[0, pl.ds(off, L), :]
        k = k_ref[0, pl.ds(off, L), :]
        v = v_ref[0, pl.ds(off, L), :]
        g = g_ref[0, :, pl.ds(off, L)] + gb_ref[...]
        li_row = jnp.sum(jnp.where(sub == head, g, 0.0), axis=0, keepdims=True)
        lf_row = _log_sigmoid(jnp.sum(jnp.where(sub == head + ML_HEADS, g, 0.0), axis=0, keepdims=True))
        lf_b = jnp.broadcast_to(lf_row, (L, L))
        bcum_col = jnp.sum(jnp.where(c <= r, lf_b, 0.0), axis=1, keepdims=True)
        lf_col = to_col(lf_row)
        li_col = to_col(li_row)
        bcum_row = jnp.sum(jnp.where(r <= c, jnp.broadcast_to(lf_col, (L, L)), 0.0), axis=0, keepdims=True)
        b_last = jnp.sum(lf_row, axis=1, keepdims=True)

        d_log = jnp.where(c <= r, bcum_col - bcum_row + li_row, -jnp.inf)
        inter = bcum_col + m_st
        m_t = jnp.maximum(inter, jnp.max(d_log, axis=1, keepdims=True))
        a_t = jnp.exp(inter - m_t)
        s_w = _dot_nt(q, k) * jnp.exp(d_log - m_t)
        num = a_t * _dot_nt(q, c_st.astype(BF16)) + _dot(s_w.astype(BF16), v)
        den = a_t * jnp.sum(q.astype(F32) * n_st, axis=1, keepdims=True) + jnp.sum(s_w, axis=1, keepdims=True)
        o_ref[0, pl.ds(off, L), :] = (num / jnp.maximum(jnp.abs(den), jnp.exp(-m_t))).astype(o_ref.dtype)

        w_end = b_last - bcum_col + li_col
        m_loc = jnp.max(w_end, axis=0, keepdims=True)
        e_end = jnp.exp(w_end - m_loc)
        c_loc = _dot_tn((e_end * v.astype(F32)).astype(BF16), k)
        n_loc = jnp.sum(e_end * k.astype(F32), axis=0, keepdims=True)
        m_new = jnp.maximum(b_last + m_st, m_loc)
        a = jnp.exp(b_last + m_st - m_new)
        gg = jnp.exp(m_loc - m_new)
        return a * c_st + gg * c_loc, a * n_st + gg * n_loc, m_new

    init = (jnp.zeros((ML_HEAD_DIM, ML_HEAD_DIM), F32), jnp.zeros((1, ML_HEAD_DIM), F32), jnp.zeros((1, 1), F32))
    lax.fori_loop(0, nc, chunk, init)


def _mlstm(qk3, pa3, grow, gbias):
    b, s, _ = pa3.shape
    vbase = (3 * SB_WIDTH + 2 * ML_WIDTH) // ML_HEAD_DIM
    return pl.pallas_call(
        _mlstm_kernel,
        grid=(b, ML_HEADS),
        in_specs=[
            pl.BlockSpec((1, s, ML_HEAD_DIM), lambda bi, h: (bi, 0, h)),
            pl.BlockSpec((1, s, ML_HEAD_DIM), lambda bi, h: (bi, 0, ML_HEADS + h)),
            pl.BlockSpec((1, s, ML_HEAD_DIM), lambda bi, h: (bi, 0, vbase + h)),
            pl.BlockSpec((1, 2 * ML_HEADS, s), lambda bi, h: (bi, 0, 0)),
            pl.BlockSpec((2 * ML_HEADS, 1), lambda bi, h: (0, 0)),
        ],
        out_specs=pl.BlockSpec((1, s, ML_HEAD_DIM), lambda bi, h: (bi, 0, h)),
        out_shape=jax.ShapeDtypeStruct((b, s, ML_WIDTH), BF16),
        compiler_params=_cparams("parallel", "parallel"),
        name="mlstm",
    )(qk3, qk3, pa3, grow, gbias)


def _merge_kernel(x_ref, osb_ref, hml_ref, mlo_ref, gate_ref, g0_ref, b0_ref, wsb_ref, wml_ref, wout_ref,
                  g1_ref, b1_ref, o_ref):
    h0 = _layer_norm(x_ref[...], g0_ref[...], b0_ref[...])
    o_ml = (_sigmoid(mlo_ref[...].astype(F32)) * hml_ref[...].astype(F32)).astype(BF16)
    y = _sigmoid(gate_ref[:, :D_MODEL].astype(F32)) * _dot(osb_ref[...], wsb_ref[...])
    y = y + _sigmoid(gate_ref[:, D_MODEL:].astype(F32)) * _dot(o_ml, wml_ref[...])
    mix = _dot(y.astype(BF16), wout_ref[...])
    o_ref[...] = _layer_norm(ALPHA * h0 + mix, g1_ref[...], b1_ref[...])


def _merge(x3, bi, osb, hml, pa, gate, g0, b0, wsb, wml, wout, g1, b1, tm=512):
    _, n, d = x3.shape
    const = lambda i: (0, 0)
    rowblk = lambda w: pl.BlockSpec((tm, w), lambda i: (i, 0))
    vec = pl.BlockSpec((1, d), const)
    mlo_blk = (3 * SB_WIDTH + 3 * ML_WIDTH) // ML_WIDTH
    return pl.pallas_call(
        _merge_kernel,
        grid=(n // tm,),
        in_specs=[
            pl.BlockSpec((None, tm, d), lambda i: (bi, i, 0)), rowblk(SB_WIDTH), rowblk(ML_WIDTH),
            pl.BlockSpec((tm, ML_WIDTH), lambda i: (i, mlo_blk)),
            rowblk(2 * d), vec, vec,
            pl.BlockSpec(wsb.shape, const), pl.BlockSpec(wml.shape, const), pl.BlockSpec(wout.shape, const),
            vec, vec,
        ],
        out_specs=rowblk(d),
        out_shape=jax.ShapeDtypeStruct((n, d), F32),
        compiler_params=_cparams("parallel"),
        name="merge_outproj_ln",
    )(x3, osb, hml, pa, gate, g0, b0, wsb, wml, wout, g1, b1)


def _topk_rows(s, k):
    n = s.shape[0]
    iota = lax.broadcasted_iota(I32, s.shape, 0)
    vals, idxs = [], []
    for _ in range(k):
        m = jnp.max(s, axis=0, keepdims=True)
        am = jnp.min(jnp.where(s == m, iota, n), axis=0, keepdims=True)
        vals.append(m)
        idxs.append(am)
        s = jnp.where(iota == am, -jnp.inf, s)
    return jnp.concatenate(vals, axis=0), jnp.concatenate(idxs, axis=0)


def _select_rows(sel, table):
    out = jnp.zeros(sel.shape, table.dtype)
    for r_ in range(table.shape[0]):
        out = jnp.where(sel == r_, table[r_:r_ + 1, :], out)
    return out


def _peer_route_kernel(h_ref, wq_ref, k1_ref, k2_ref, idx_ref, gate_ref, q_scr, *, tt):
    q_scr[...] = _dot(h_ref[...].astype(BF16), wq_ref[...]).astype(BF16)
    nsub = h_ref.shape[0] // tt

    def one(it, _):
        head = it % PEER_HEADS
        sub = it // PEER_HEADS
        roff = pl.multiple_of(sub * tt, tt)
        coff = pl.multiple_of(head * PEER_QDIM, PEER_QDIM)
        q1 = q_scr[pl.ds(roff, tt), pl.ds(coff, PEER_HALF)]
        q2 = q_scr[pl.ds(roff, tt), pl.ds(coff + PEER_HALF, PEER_HALF)]
        v1, i1 = _topk_rows(_dot_nt(k1_ref[...], q1), PEER_TOPK)
        v2, i2 = _topk_rows(_dot_nt(k2_ref[...], q2), PEER_TOPK)
        cand = jnp.concatenate([v1[i:i + 1, :] + v2 for i in range(PEER_TOPK)], axis=0)
        top_s, pos = _topk_rows(cand, PEER_TOPK)
        e1 = _select_rows(pos // PEER_TOPK, i1)
        e2 = _select_rows(pos % PEER_TOPK, i2)
        ex = jnp.exp(top_s - top_s[0:1, :])
        gates = ex / jnp.sum(ex, axis=0, keepdims=True)
        hoff = pl.multiple_of(head * PEER_TOPK, PEER_TOPK)
        idx_ref[pl.ds(hoff, PEER_TOPK), pl.ds(roff, tt)] = e1 * PEER_KEYS + e2
        gate_ref[pl.ds(hoff, PEER_TOPK), pl.ds(roff, tt)] = gates
        return 0

    lax.fori_loop(0, nsub * PEER_HEADS, one, 0)


def _peer_route(h1, tok0, n, wq, k1, k2, tm=256, tt=128):
    d = h1.shape[1]
    off = tok0 // tm
    const = lambda i: (0, 0)
    nsel = PEER_HEADS * PEER_TOPK
    return pl.pallas_call(
        functools.partial(_peer_route_kernel, tt=tt),
        grid=(n // tm,),
        in_specs=[
            pl.BlockSpec((tm, d), lambda i: (off + i, 0)),
            pl.BlockSpec(wq.shape, const),
            pl.BlockSpec(k1.shape, const),
            pl.BlockSpec(k2.shape, const),
        ],
        out_specs=[pl.BlockSpec((nsel, tm), lambda i: (0, i)), pl.BlockSpec((nsel, tm), lambda i: (0, i))],
        out_shape=[jax.ShapeDtypeStruct((nsel, n), I32), jax.ShapeDtypeStruct((nsel, n), F32)],
        scratch_shapes=[pltpu.VMEM((tm, PEER_HEADS * PEER_QDIM), BF16)],
        compiler_params=_cparams("parallel"),
        name="peer_route",
    )(h1, wq, k1, k2)


def _dot_split3(x, sel):
    hi = x.astype(BF16)
    r1 = x - hi.astype(F32)
    mid = r1.astype(BF16)
    lo = (r1 - mid.astype(F32)).astype(BF16)
    return _dot(hi, sel) + _dot(mid, sel) + _dot(lo, sel)


def _gelu_gate_kernel(part_ref, gate_ref, sel_ref, selt_ref, o_ref):
    a = _dot_split3(part_ref[...], sel_ref[...])
    w = gate_ref[...] * (0.5 * a * (1.0 + lax.erf(a * (2.0 ** -0.5))))
    o_ref[...] = _dot_split3(w, selt_ref[...])


def _gelu_gate(part, gates, tm=512):
    n, wide = part.shape
    nsel = gates.shape[1]
    lanes = wide // nsel
    sel = (jnp.arange(wide)[:, None] // lanes == jnp.arange(nsel)[None, :]).astype(BF16)
    const = lambda i: (0, 0)
    return pl.pallas_call(
        _gelu_gate_kernel,
        grid=(n // tm,),
        in_specs=[pl.BlockSpec((tm, wide), lambda i: (i, 0)), pl.BlockSpec((tm, nsel), lambda i: (i, 0)),
                  pl.BlockSpec((wide, nsel), const), pl.BlockSpec((nsel, wide), const)],
        out_specs=pl.BlockSpec((tm, wide), lambda i: (i, 0)),
        out_shape=jax.ShapeDtypeStruct((n, wide), F32),
        compiler_params=_cparams("parallel"),
        name="gelu_gate",
    )(part, gates, sel, sel.T)


def _final_kernel(h_ref, ffn_ref, p_ref, wg_ref, wp_ref, g_ref, b_ref, o_ref):
    h = h_ref[...]
    ple = _sigmoid(_dot(h.astype(BF16), wg_ref[...])) * _dot(p_ref[...].astype(BF16), wp_ref[...])
    o_ref[...] = _layer_norm(ALPHA * h + ffn_ref[...] + ple, g_ref[...], b_ref[...])


def _final(h1, tok0, ffn, p4, bi, wg, wp, g, b, tm=512):
    n, d = ffn.shape
    off = tok0 // tm
    const = lambda i: (0, 0)
    rowblk = lambda w: pl.BlockSpec((tm, w), lambda i: (i, 0))
    vec = pl.BlockSpec((1, d), const)
    return pl.pallas_call(
        _final_kernel,
        grid=(n // tm,),
        in_specs=[pl.BlockSpec((tm, d), lambda i: (off + i, 0)), rowblk(d),
                  pl.BlockSpec((None, None, tm, p4.shape[3]), lambda i: (0, bi, off + i, 0)),
                  pl.BlockSpec(wg.shape, const), pl.BlockSpec(wp.shape, const), vec, vec],
        out_specs=rowblk(d),
        out_shape=jax.ShapeDtypeStruct((n, d), F32),
        compiler_params=_cparams("parallel"),
        name="ple_final_ln",
    )(h1, ffn, p4, wg, wp, g, b)


SC_CORES = 2
SC_SUBCORES = 16
SC_LANES = 16
SC_WORKERS = SC_CORES * SC_SUBCORES
PEER_NSEL = PEER_HEADS * PEER_TOPK
SC_ROWS = 16
SC_NBUF = 4
SC_GROUP = 8


def _sc_mesh():
    return plsc.VectorSubcoreMesh(core_axis_name="c", subcore_axis_name="s", num_cores=SC_CORES,
                                  num_subcores=SC_SUBCORES)


def _sc_grouped_pipeline(ngroups, stage_copies, out_copy, tab_hbm, idx_v, buf, sems, begin_group, compute):
    nchunk = PEER_NSEL // SC_ROWS
    nsteps = SC_GROUP * nchunk
    assert nsteps % SC_NBUF == 0 and ngroups % 2 == 0

    def gather(gs, step, slot):
        t = step // nchunk
        c = step % nchunk
        return pltpu.make_async_copy(tab_hbm.at[idx_v.at[gs, t, pl.ds(c * SC_ROWS, SC_ROWS)]], buf.at[slot],
                                     sems.at[slot])

    for cp in stage_copies(0, 0):
        cp.start()
    for cp in stage_copies(0, 0):
        cp.wait()
    for b in range(SC_NBUF - 1):
        gather(0, b, b).start()
    for cp in stage_copies(1, 1):
        cp.start()

    def group_pair(gp, _):
        for gs in range(2):
            g = 2 * gp + gs

            @pl.when(g >= 2)
            def _():
                out_copy(g - 2, gs).wait()

            begin_group(gs)

            def ring_round(q, _):
                for b in range(SC_NBUF):
                    s = q * SC_NBUF + b
                    ahead = s + SC_NBUF - 1
                    slot_a = (b + SC_NBUF - 1) % SC_NBUF

                    @pl.when(ahead < nsteps)
                    def _():
                        gather(gs, ahead, slot_a).start()

                    if b > 0:
                        @pl.when(jnp.logical_and(ahead >= nsteps, g + 1 < ngroups))
                        def _():
                            if b == 1:
                                for cp in stage_copies(g + 1, 1 - gs):
                                    cp.wait()
                            gather(1 - gs, ahead - nsteps, slot_a).start()

                    gather(gs, s, b).wait()
                    compute(gs, s // nchunk, s % nchunk, b)
                return 0

            lax.fori_loop(0, nsteps // SC_NBUF, ring_round, 0)
            out_copy(g, gs).start()

            @pl.when(g + 2 < ngroups)
            def _():
                for cp in stage_copies(g + 2, gs):
                    cp.start()
        return 0

    lax.fori_loop(0, ngroups // 2, group_pair, 0)
    out_copy(ngroups - 2, 0).wait()
    out_copy(ngroups - 1, 1).wait()


def _peer_act_sc(h1, tok0, idx, u_tab):
    n = idx.shape[0]
    d = h1.shape[1]
    per_w = n // SC_WORKERS
    hold = 32

    @functools.partial(
        pl.kernel, mesh=_sc_mesh(),
        out_type=jax.ShapeDtypeStruct((n, PEER_NSEL * SC_LANES), F32),
        scratch_types=[
            pltpu.VMEM((2, SC_GROUP, d), F32),
            pltpu.VMEM((2, SC_GROUP, PEER_NSEL), I32),
            pltpu.VMEM((2, SC_GROUP, PEER_NSEL * SC_LANES), F32),
            pltpu.VMEM((SC_NBUF, SC_ROWS, d), F32),
            pltpu.SemaphoreType.DMA((SC_NBUF,)),
            pltpu.SemaphoreType.DMA((2,)),
            pltpu.SemaphoreType.DMA((2,)),
            pltpu.SemaphoreType.DMA((2,)),
        ],
        name="peer_act_sc",
    )
    def k(h_hbm, idx_hbm, u_hbm, out_hbm, h_v, idx_v, out_v, buf, sems, sem_h, sem_i, sem_o):
        base = (lax.axis_index("s") * SC_CORES + lax.axis_index("c")) * per_w

        def stage_copies(g, gs):
            row0 = base + g * SC_GROUP
            return (pltpu.make_async_copy(h_hbm.at[pl.ds(tok0 + row0, SC_GROUP)], h_v.at[gs], sem_h.at[gs]),
                    pltpu.make_async_copy(idx_hbm.at[pl.ds(row0, SC_GROUP)], idx_v.at[gs], sem_i.at[gs]))

        def out_copy(g, gs):
            return pltpu.make_async_copy(out_v.at[gs], out_hbm.at[pl.ds(base + g * SC_GROUP, SC_GROUP)],
                                         sem_o.at[gs])

        def compute(gs, t, c, slot):
            for jb in range(d // (hold * SC_LANES)):
                col0 = jb * hold * SC_LANES
                hv = [h_v[gs, t, pl.ds(col0 + jj * SC_LANES, SC_LANES)] for jj in range(hold)]

                @plsc.parallel_loop(0, SC_ROWS)
                def _(r):
                    ps = [buf[slot, r, pl.ds(col0 + jj * SC_LANES, SC_LANES)] * hv[jj] for jj in range(hold)]
                    while len(ps) > 1:
                        ps = [ps[i] + ps[i + 1] for i in range(0, len(ps), 2)]
                    dst = out_v.at[gs, t, pl.ds((c * SC_ROWS + r) * SC_LANES, SC_LANES)]
                    if jb == 0:
                        dst[...] = ps[0]
                    else:
                        plsc.addupdate(dst, ps[0])

        _sc_grouped_pipeline(per_w // SC_GROUP, stage_copies, out_copy, u_hbm, idx_v, buf, sems, lambda gs: None,
                             compute)

    return k(h1, idx, u_tab)


def _peer_combine_sc(wexp, idx, v_tab):
    n = idx.shape[0]
    d = v_tab.shape[1]
    per_w = n // SC_WORKERS
    nvec = d // SC_LANES
    ncol = 4

    @functools.partial(
        pl.kernel, mesh=_sc_mesh(),
        out_type=jax.ShapeDtypeStruct((n, d), F32),
        scratch_types=[
            pltpu.VMEM((2, SC_GROUP, PEER_NSEL * SC_LANES), F32),
            pltpu.VMEM((2, SC_GROUP, PEER_NSEL), I32),
            pltpu.VMEM((2, SC_GROUP, d), F32),
            pltpu.VMEM((SC_NBUF, SC_ROWS, d), F32),
            pltpu.SemaphoreType.DMA((SC_NBUF,)),
            pltpu.SemaphoreType.DMA((2,)),
            pltpu.SemaphoreType.DMA((2,)),
            pltpu.SemaphoreType.DMA((2,)),
        ],
        name="peer_combine_sc",
    )
    def k(w_hbm, idx_hbm, v_hbm, out_hbm, w_v, idx_v, o_v, buf, sems, sem_w, sem_i, sem_o):
        base = (lax.axis_index("s") * SC_CORES + lax.axis_index("c")) * per_w

        def stage_copies(g, gs):
            row0 = base + g * SC_GROUP
            return (pltpu.make_async_copy(w_hbm.at[pl.ds(row0, SC_GROUP)], w_v.at[gs], sem_w.at[gs]),
                    pltpu.make_async_copy(idx_hbm.at[pl.ds(row0, SC_GROUP)], idx_v.at[gs], sem_i.at[gs]))

        def out_copy(g, gs):
            return pltpu.make_async_copy(o_v.at[gs], out_hbm.at[pl.ds(base + g * SC_GROUP, SC_GROUP)], sem_o.at[gs])

        def begin_group(gs):
            def zero(i, _):
                o_v[gs, i // nvec, pl.ds((i % nvec) * SC_LANES, SC_LANES)] = jnp.zeros((SC_LANES,), F32)
                return 0

            lax.fori_loop(0, SC_GROUP * nvec, zero, 0)

        def compute(gs, t, c, slot):
            @plsc.parallel_loop(0, nvec // ncol, unroll=2)
            def _(cb):
                col0 = cb * ncol * SC_LANES
                acc = [None] * ncol
                for r in range(SC_ROWS):
                    w = w_v[gs, t, pl.ds((c * SC_ROWS + r) * SC_LANES, SC_LANES)]
                    for kk in range(ncol):
                        x = w * buf[slot, r, pl.ds(col0 + kk * SC_LANES, SC_LANES)]
                        acc[kk] = x if acc[kk] is None else acc[kk] + x
                for kk in range(ncol):
                    plsc.addupdate(o_v.at[gs, t, pl.ds(col0 + kk * SC_LANES, SC_LANES)], acc[kk])

        _sc_grouped_pipeline(per_w // SC_GROUP, stage_copies, out_copy, v_hbm, idx_v, buf, sems, begin_group, compute)

    return k(wexp, idx, v_tab)


def _token_mixer_and_norm(x3, bi, g0, b0, wa, wif, wg, gbias, conv_w, conv_b, wsb, wml, wout, g1, b1):
    s = x3.shape[1]
    n_if = 2 * ML_HEADS
    pa, gif, gate = _inproj(x3, bi, g0, b0, wa, wif, wg)
    pa3 = pa.reshape(1, s, PA_WIDTH)
    o_sb = _sb_attention(pa3)
    qk3 = _conv_silu(pa3, conv_w, conv_b)
    grow = jnp.swapaxes(gif.reshape(1, s, LANES)[:, :, :n_if], 1, 2)
    h_ml = _mlstm(qk3, pa3, grow, gbias)
    return _merge(x3, bi, o_sb.reshape(s, SB_WIDTH), h_ml.reshape(s, ML_WIDTH), pa, gate, g0, b0, wsb, wml, wout,
                  g1, b1)


def _after(value, prev):
    if prev is None:
        return value
    return lax.optimization_barrier((value, prev))[0]


def kernel(x, p, ln0_g, ln0_b, w_in, b_igate, b_fgate, conv_w, conv_b, w_branch_sb, w_branch_ml, w_out, ln1_g, ln1_b, peer_wq, peer_k1, peer_k2, peer_u, peer_v, w_ple_gate, w_ple, ln2_g, ln2_b):
    b, s, d = x.shape
    assert w_in.shape[0] == DEPTH
    i = 0
    row = lambda v: v.reshape(1, -1)
    n_if = 2 * ML_HEADS
    wa = w_in[i][:, :PA_WIDTH].astype(BF16)
    wif = jnp.pad(w_in[i][:, PA_WIDTH:PA_WIDTH + n_if], ((0, 0), (0, LANES - n_if))).astype(BF16)
    wg = w_in[i][:, PA_WIDTH + n_if:].astype(BF16)
    gbias = jnp.concatenate([b_igate[i], b_fgate[i]]).reshape(n_if, 1)
    wsb, wml, wout = w_branch_sb[i].astype(BF16), w_branch_ml[i].astype(BF16), w_out[i].astype(BF16)
    wq, k1, k2 = peer_wq[i].astype(BF16), peer_k1[i].astype(BF16), peer_k2[i].astype(BF16)
    wpg, wp = w_ple_gate[i].astype(BF16), w_ple[i].astype(BF16)

    halves = 2
    sp = s // halves
    h1s, routed, parts, ffns, outs = {}, {}, {}, {}, {}
    last = None

    def mix(e):
        h1s[e] = _token_mixer_and_norm(x, e, _after(row(ln0_g), last), row(ln0_b), wa, wif, wg, gbias, conv_w[i],
                                       row(conv_b[i]), wsb, wml, wout, row(ln1_g[i]), row(ln1_b[i]))
        return h1s[e]

    def route(pc):
        e, hf = divmod(pc, halves)
        idx_t, gate_t = _peer_route(_after(h1s[e], last), hf * sp, sp, wq, k1, k2)
        idx = idx_t.T
        routed[pc] = (idx, gate_t.T)
        parts[pc] = _peer_act_sc(h1s[e], hf * sp, _after(idx, ffns.get(pc - 2)), peer_u[i])
        return idx_t

    def gelu(pc):
        e, hf = divmod(pc, halves)
        idx, gates = routed[pc]
        wexp = _gelu_gate(parts[pc], _after(gates, last))
        ffns[pc] = _peer_combine_sc(wexp, idx, peer_v[i])
        outs[pc] = _final(h1s[e], hf * sp, ffns[pc], p, e, wpg, wp, row(ln2_g[i]), row(ln2_b[i]))
        return wexp

    order = [(mix, 0), (route, 0), (route, 1)]
    for e in range(1, b):
        order += [(mix, e), (gelu, 2 * e - 2), (route, 2 * e), (gelu, 2 * e - 1), (route, 2 * e + 1)]
    order += [(gelu, 2 * b - 2), (gelu, 2 * b - 1)]
    for stage, arg in order:
        last = stage(arg)
    return jnp.concatenate([outs[pc] for pc in range(halves * b)]).reshape(b, s, d)
```

```python
import functools

import jax
import jax.numpy as jnp
from jax import lax
from jax.experimental import pallas as pl
from jax.experimental.pallas import tpu as pltpu
from jax.experimental.pallas import tpu_sc as plsc

F32 = jnp.float32
BF16 = jnp.bfloat16
I32 = jnp.int32

D_MODEL = 1024
SB_HEADS = 8
SB_HEAD_DIM = 64
SB_WIDTH = SB_HEADS * SB_HEAD_DIM
ML_HEADS = 4
ML_HEAD_DIM = 128
ML_WIDTH = ML_HEADS * ML_HEAD_DIM
ML_CHUNK = 128
CONV_WIDTH = 4
PEER_HEADS = 8
PEER_KEYS = 128
PEER_QDIM = 256
PEER_HALF = PEER_QDIM // 2
PEER_TOPK = 16
PEER_BLOCK = 128
DEPTH = 1
ALPHA = (2.0 * DEPTH) ** 0.25
LN_EPS = 1e-5

LANES = 128
VMEM_LIMIT = 56 * 1024 * 1024
SB_SKIP_LOG = -104.0

PA_WIDTH = 3 * SB_WIDTH + 4 * ML_WIDTH


def _cparams(*sem):
    return pltpu.CompilerParams(dimension_semantics=sem, vmem_limit_bytes=VMEM_LIMIT)


def _layer_norm(x, g, b):
    mu = jnp.mean(x, axis=-1, keepdims=True)
    xc = x - mu
    var = jnp.mean(xc * xc, axis=-1, keepdims=True)
    return xc * lax.rsqrt(var + LN_EPS) * g + b


def _log_sigmoid(z):
    return jnp.minimum(z, 0.0) - jnp.log1p(jnp.exp(-jnp.abs(z)))


def _sigmoid(z):
    return 1.0 / (1.0 + jnp.exp(-z))


def _dot(a, b):
    return jnp.dot(a, b, preferred_element_type=F32)


def _dot_nt(a, b):
    return lax.dot_general(a, b, (((1,), (1,)), ((), ())), preferred_element_type=F32)


def _dot_tn(a, b):
    return lax.dot_general(a, b, (((0,), (0,)), ((), ())), preferred_element_type=F32)


def _inproj_kernel(x_ref, g_ref, b_ref, wa_ref, wif_ref, wg_ref, oa_ref, oif_ref, og_ref, *, cw):
    h = _layer_norm(x_ref[...], g_ref[...], b_ref[...]).astype(BF16)
    for j in range(0, wa_ref.shape[1], cw):
        oa_ref[:, j:j + cw] = _dot(h, wa_ref[:, j:j + cw]).astype(BF16)
    oif_ref[...] = _dot(h, wif_ref[...])
    for j in range(0, wg_ref.shape[1], cw):
        og_ref[:, j:j + cw] = _dot(h, wg_ref[:, j:j + cw]).astype(BF16)


def _inproj(x3, bi, g, b, wa, wif, wg, tm=512):
    _, n, d = x3.shape
    const = lambda i: (0, 0)
    return pl.pallas_call(
        functools.partial(_inproj_kernel, cw=512),
        grid=(n // tm,),
        in_specs=[
            pl.BlockSpec((None, tm, d), lambda i: (bi, i, 0)),
            pl.BlockSpec((1, d), const),
            pl.BlockSpec((1, d), const),
            pl.BlockSpec(wa.shape, const),
            pl.BlockSpec(wif.shape, const),
            pl.BlockSpec(wg.shape, const),
        ],
        out_specs=[
            pl.BlockSpec((tm, wa.shape[1]), lambda i: (i, 0)),
            pl.BlockSpec((tm, wif.shape[1]), lambda i: (i, 0)),
            pl.BlockSpec((tm, wg.shape[1]), lambda i: (i, 0)),
        ],
        out_shape=[
            jax.ShapeDtypeStruct((n, wa.shape[1]), BF16),
            jax.ShapeDtypeStruct((n, wif.shape[1]), F32),
            jax.ShapeDtypeStruct((n, wg.shape[1]), BF16),
        ],
        compiler_params=_cparams("parallel"),
        name="ln_inproj",
    )(x3, g, b, wa, wif, wg)


def _sb_kernel(q_ref, k_ref, v_ref, o_ref, *, tq):
    qi = pl.program_id(2)
    q = q_ref[0]
    lane = lax.broadcasted_iota(I32, (1, LANES), 1)
    row = lax.broadcasted_iota(I32, (tq, tq), 0)
    col = lax.broadcasted_iota(I32, (tq, tq), 1)
    later = (row > col).astype(BF16)
    scale = SB_HEAD_DIM ** -0.5
    first = lane < SB_HEAD_DIM
    zero = jnp.zeros_like(q)
    q2 = jnp.concatenate([jnp.where(first, q, zero), jnp.where(first, zero, q)], axis=0)
    q2 = (q2.astype(F32) * scale).astype(BF16)
    row2 = lax.broadcasted_iota(I32, (2 * tq, tq), 0)
    causal = lax.broadcasted_iota(I32, (2 * tq, tq), 1) < jnp.where(row2 >= tq, row2 - tq, row2)

    def tile(qh, kb, carry, masked):
        off = pl.multiple_of(kb * tq, tq)
        k_blk = k_ref[0, pl.ds(off, tq), :]
        v_blk = v_ref[0, pl.ds(off, tq), :]
        z = _dot_nt(qh, k_blk)
        lb = _log_sigmoid(z)
        lom = lb - z
        if masked:
            lom = jnp.where(causal, lom, 0.0)
        hi = lom.astype(BF16)
        lo = (lom - hi.astype(F32)).astype(BF16)
        tail = _dot(hi, later) + _dot(lo, later) + carry
        w = jnp.exp(lb + tail)
        if masked:
            w = jnp.where(causal, w, 0.0)
        contrib = _dot(w.astype(BF16), v_blk)
        return contrib, carry + jnp.sum(lom, axis=1, keepdims=True)

    acc, carry = tile(q2, qi, jnp.zeros((2 * tq, 1), F32), True)

    def cond(st):
        kb, _, _, cmax = st
        return jnp.logical_and(kb >= 0, cmax > SB_SKIP_LOG)

    def body(st):
        kb, acc, carry, _ = st
        contrib, carry = tile(q2, kb, carry, False)
        return kb - 1, acc + contrib, carry, jnp.max(carry)

    _, acc, _, _ = lax.while_loop(cond, body, (qi - 1, acc, carry, jnp.max(carry)))
    o_ref[0] = jnp.where(first, acc[:tq], acc[tq:]).astype(o_ref.dtype)


def _sb_attention(pa3, tq=128):
    b, s, _ = pa3.shape
    npair = SB_WIDTH // LANES
    return pl.pallas_call(
        functools.partial(_sb_kernel, tq=tq),
        grid=(b, npair, s // tq),
        in_specs=[
            pl.BlockSpec((1, tq, LANES), lambda bi, hp, qi: (bi, qi, hp)),
            pl.BlockSpec((1, s, LANES), lambda bi, hp, qi: (bi, 0, npair + hp)),
            pl.BlockSpec((1, s, LANES), lambda bi, hp, qi: (bi, 0, 2 * npair + hp)),
        ],
        out_specs=pl.BlockSpec((1, tq, LANES), lambda bi, hp, qi: (bi, qi, hp)),
        out_shape=jax.ShapeDtypeStruct((b, s, SB_WIDTH), BF16),
        compiler_params=_cparams("parallel", "parallel", "arbitrary"),
        name="sb_attention",
    )(pa3, pa3, pa3)


CONV_HALO = 16


def _conv_kernel(x_ref, prev_ref, w_ref, b_ref, o_ref, buf_ref, *, ts):
    si = pl.program_id(1)
    is_k = pl.program_id(2)
    prev = prev_ref[0].astype(F32)
    buf_ref[0:CONV_HALO, :] = jnp.where(si == 0, 0.0, prev)
    buf_ref[CONV_HALO:, :] = x_ref[0].astype(F32)
    y = b_ref[...] + w_ref[0:1, :] * buf_ref[CONV_HALO:, :]
    for j in range(1, CONV_WIDTH):
        y = y + w_ref[j:j + 1, :] * buf_ref[CONV_HALO - j:CONV_HALO - j + ts, :]
    y = y * _sigmoid(y)
    y = y * jnp.where(is_k == 1, ML_HEAD_DIM ** -0.5, 1.0)
    o_ref[0] = y.astype(o_ref.dtype)


def _conv_silu(pa3, conv_w, conv_b, ts=1024):
    b, s, _ = pa3.shape
    ts = min(ts, s)
    cb = ML_WIDTH
    base = 3 * SB_WIDTH // cb
    hb = ts // CONV_HALO
    return pl.pallas_call(
        functools.partial(_conv_kernel, ts=ts),
        grid=(b, s // ts, 2),
        in_specs=[
            pl.BlockSpec((1, ts, cb), lambda bi, si, j: (bi, si, base + j)),
            pl.BlockSpec((1, CONV_HALO, cb), lambda bi, si, j: (bi, jnp.maximum(si * hb - 1, 0), base + j)),
            pl.BlockSpec((CONV_WIDTH, cb), lambda bi, si, j: (0, j)),
            pl.BlockSpec((1, cb), lambda bi, si, j: (0, j)),
        ],
        out_specs=pl.BlockSpec((1, ts, cb), lambda bi, si, j: (bi, si, j)),
        out_shape=jax.ShapeDtypeStruct((b, s, 2 * ML_WIDTH), BF16),
        scratch_shapes=[pltpu.VMEM((ts + CONV_HALO, cb), F32)],
        compiler_params=_cparams("parallel", "parallel", "parallel"),
        name="conv_silu",
    )(pa3, pa3, conv_w, conv_b)


def _mlstm_kernel(q_ref, k_ref, v_ref, g_ref, gb_ref, o_ref):
    L = ML_CHUNK
    head = pl.program_id(1)
    nc = q_ref.shape[1] // L
    r = lax.broadcasted_iota(I32, (L, L), 0)
    c = lax.broadcasted_iota(I32, (L, L), 1)
    eye = r == c
    sub = lax.broadcasted_iota(I32, (2 * ML_HEADS, L), 0)

    def to_col(row):
        return jnp.sum(jnp.where(eye, jnp.broadcast_to(row, (L, L)), 0.0), axis=1, keepdims=True)

    def chunk(ci, st):
        c_st, n_st, m_st = st
        off = pl.multiple_of(ci * L, L)
        q = q_ref[0, pl.ds(off, L), :]
        k = k_ref[0, pl.ds(off, L), :]
        v = v_ref[0, pl.ds(off, L), :]
        g = g_ref[0, :, pl.ds(off, L)] + gb_ref[...]
        li_row = jnp.sum(jnp.where(sub == head, g, 0.0), axis=0, keepdims=True)
        lf_row = _log_sigmoid(jnp.sum(jnp.where(sub == head + ML_HEADS, g, 0.0), axis=0, keepdims=True))
        lf_b = jnp.broadcast_to(lf_row, (L, L))
        bcum_col = jnp.sum(jnp.where(c <= r, lf_b, 0.0), axis=1, keepdims=True)
        lf_col = to_col(lf_row)
        li_col = to_col(li_row)
        bcum_row = jnp.sum(jnp.where(r <= c, jnp.broadcast_to(lf_col, (L, L)), 0.0), axis=0, keepdims=True)
        b_last = jnp.sum(lf_row, axis=1, keepdims=True)

        d_log = jnp.where(c <= r, bcum_col - bcum_row + li_row, -jnp.inf)
        inter = bcum_col + m_st
        m_t = jnp.maximum(inter, jnp.max(d_log, axis=1, keepdims=True))
        a_t = jnp.exp(inter - m_t)
        s_w = _dot_nt(q, k) * jnp.exp(d_log - m_t)
        num = a_t * _dot_nt(q, c_st.astype(BF16)) + _dot(s_w.astype(BF16), v)
        den = a_t * jnp.sum(q.astype(F32) * n_st, axis=1, keepdims=True) + jnp.sum(s_w, axis=1, keepdims=True)
        o_ref[0, pl.ds(off, L), :] = (num / jnp.maximum(jnp.abs(den), jnp.exp(-m_t))).astype(o_ref.dtype)

        w_end = b_last - bcum_col + li_col
        m_loc = jnp.max(w_end, axis=0, keepdims=True)
        e_end = jnp.exp(w_end - m_loc)
        c_loc = _dot_tn((e_end * v.astype(F32)).astype(BF16), k)
        n_loc = jnp.sum(e_end * k.astype(F32), axis=0, keepdims=True)
        m_new = jnp.maximum(b_last + m_st, m_loc)
        a = jnp.exp(b_last + m_st - m_new)
        gg = jnp.exp(m_loc - m_new)
        return a * c_st + gg * c_loc, a * n_st + gg * n_loc, m_new

    init = (jnp.zeros((ML_HEAD_DIM, ML_HEAD_DIM), F32), jnp.zeros((1, ML_HEAD_DIM), F32), jnp.zeros((1, 1), F32))
    lax.fori_loop(0, nc, chunk, init)


def _mlstm(qk3, pa3, grow, gbias):
    b, s, _ = pa3.shape
    vbase = (3 * SB_WIDTH + 2 * ML_WIDTH) // ML_HEAD_DIM
    return pl.pallas_call(
        _mlstm_kernel,
        grid=(b, ML_HEADS),
        in_specs=[
            pl.BlockSpec((1, s, ML_HEAD_DIM), lambda bi, h: (bi, 0, h)),
            pl.BlockSpec((1, s, ML_HEAD_DIM), lambda bi, h: (bi, 0, ML_HEADS + h)),
            pl.BlockSpec((1, s, ML_HEAD_DIM), lambda bi, h: (bi, 0, vbase + h)),
            pl.BlockSpec((1, 2 * ML_HEADS, s), lambda bi, h: (bi, 0, 0)),
            pl.BlockSpec((2 * ML_HEADS, 1), lambda bi, h: (0, 0)),
        ],
        out_specs=pl.BlockSpec((1, s, ML_HEAD_DIM), lambda bi, h: (bi, 0, h)),
        out_shape=jax.ShapeDtypeStruct((b, s, ML_WIDTH), BF16),
        compiler_params=_cparams("parallel", "parallel"),
        name="mlstm",
    )(qk3, qk3, pa3, grow, gbias)


def _merge_kernel(x_ref, osb_ref, hml_ref, mlo_ref, gate_ref, g0_ref, b0_ref, wsb_ref, wml_ref, wout_ref,
                  g1_ref, b1_ref, o_ref):
    h0 = _layer_norm(x_ref[...], g0_ref[...], b0_ref[...])
    o_ml = (_sigmoid(mlo_ref[...].astype(F32)) * hml_ref[...].astype(F32)).astype(BF16)
    y = _sigmoid(gate_ref[:, :D_MODEL].astype(F32)) * _dot(osb_ref[...], wsb_ref[...])
    y = y + _sigmoid(gate_ref[:, D_MODEL:].astype(F32)) * _dot(o_ml, wml_ref[...])
    mix = _dot(y.astype(BF16), wout_ref[...])
    o_ref[...] = _layer_norm(ALPHA * h0 + mix, g1_ref[...], b1_ref[...])


def _merge(x3, bi, osb, hml, pa, gate, g0, b0, wsb, wml, wout, g1, b1, tm=512):
    _, n, d = x3.shape
    const = lambda i: (0, 0)
    rowblk = lambda w: pl.BlockSpec((tm, w), lambda i: (i, 0))
    vec = pl.BlockSpec((1, d), const)
    mlo_blk = (3 * SB_WIDTH + 3 * ML_WIDTH) // ML_WIDTH
    return pl.pallas_call(
        _merge_kernel,
        grid=(n // tm,),
        in_specs=[
            pl.BlockSpec((None, tm, d), lambda i: (bi, i, 0)), rowblk(SB_WIDTH), rowblk(ML_WIDTH),
            pl.BlockSpec((tm, ML_WIDTH), lambda i: (i, mlo_blk)),
            rowblk(2 * d), vec, vec,
            pl.BlockSpec(wsb.shape, const), pl.BlockSpec(wml.shape, const), pl.BlockSpec(wout.shape, const),
            vec, vec,
        ],
        out_specs=rowblk(d),
        out_shape=jax.ShapeDtypeStruct((n, d), F32),
        compiler_params=_cparams("parallel"),
        name="merge_outproj_ln",
    )(x3, osb, hml, pa, gate, g0, b0, wsb, wml, wout, g1, b1)


def _topk_rows(s, k):
    n = s.shape[0]
    iota = lax.broadcasted_iota(I32, s.shape, 0)
    vals, idxs = [], []
    for _ in range(k):
        m = jnp.max(s, axis=0, keepdims=True)
        am = jnp.min(jnp.where(s == m, iota, n), axis=0, keepdims=True)
        vals.append(m)
        idxs.append(am)
        s = jnp.where(iota == am, -jnp.inf, s)
    return jnp.concatenate(vals, axis=0), jnp.concatenate(idxs, axis=0)


def _select_rows(sel, table):
    out = jnp.zeros(sel.shape, table.dtype)
    for r_ in range(table.shape[0]):
        out = jnp.where(sel == r_, table[r_:r_ + 1, :], out)
    return out


def _peer_route_kernel(h_ref, wq_ref, k1_ref, k2_ref, idx_ref, gate_ref, q_scr, *, tt):
    q_scr[...] = _dot(h_ref[...].astype(BF16), wq_ref[...]).astype(BF16)
    nsub = h_ref.shape[0] // tt

    def one(it, _):
        head = it % PEER_HEADS
        sub = it // PEER_HEADS
        roff = pl.multiple_of(sub * tt, tt)
        coff = pl.multiple_of(head * PEER_QDIM, PEER_QDIM)
        q1 = q_scr[pl.ds(roff, tt), pl.ds(coff, PEER_HALF)]
        q2 = q_scr[pl.ds(roff, tt), pl.ds(coff + PEER_HALF, PEER_HALF)]
        v1, i1 = _topk_rows(_dot_nt(k1_ref[...], q1), PEER_TOPK)
        v2, i2 = _topk_rows(_dot_nt(k2_ref[...], q2), PEER_TOPK)
        cand = jnp.concatenate([v1[i:i + 1, :] + v2 for i in range(PEER_TOPK)], axis=0)
        top_s, pos = _topk_rows(cand, PEER_TOPK)
        e1 = _select_rows(pos // PEER_TOPK, i1)
        e2 = _select_rows(pos % PEER_TOPK, i2)
        ex = jnp.exp(top_s - top_s[0:1, :])
        gates = ex / jnp.sum(ex, axis=0, keepdims=True)
        hoff = pl.multiple_of(head * PEER_TOPK, PEER_TOPK)
        idx_ref[pl.ds(hoff, PEER_TOPK), pl.ds(roff, tt)] = e1 * PEER_KEYS + e2
        gate_ref[pl.ds(hoff, PEER_TOPK), pl.ds(roff, tt)] = gates
        return 0

    lax.fori_loop(0, nsub * PEER_HEADS, one, 0)


def _peer_route(h1, tok0, n, wq, k1, k2, tm=256, tt=128):
    d = h1.shape[1]
    off = tok0 // tm
    const = lambda i: (0, 0)
    nsel = PEER_HEADS * PEER_TOPK
    return pl.pallas_call(
        functools.partial(_peer_route_kernel, tt=tt),
        grid=(n // tm,),
        in_specs=[
            pl.BlockSpec((tm, d), lambda i: (off + i, 0)),
            pl.BlockSpec(wq.shape, const),
            pl.BlockSpec(k1.shape, const),
            pl.BlockSpec(k2.shape, const),
        ],
        out_specs=[pl.BlockSpec((nsel, tm), lambda i: (0, i)), pl.BlockSpec((nsel, tm), lambda i: (0, i))],
        out_shape=[jax.ShapeDtypeStruct((nsel, n), I32), jax.ShapeDtypeStruct((nsel, n), F32)],
        scratch_shapes=[pltpu.VMEM((tm, PEER_HEADS * PEER_QDIM), BF16)],
        compiler_params=_cparams("parallel"),
        name="peer_route",
    )(h1, wq, k1, k2)


def _dot_split3(x, sel):
    hi = x.astype(BF16)
    r1 = x - hi.astype(F32)
    mid = r1.astype(BF16)
    lo = (r1 - mid.astype(F32)).astype(BF16)
    return _dot(hi, sel) + _dot(mid, sel) + _dot(lo, sel)


def _gelu_gate_kernel(part_ref, gate_ref, sel_ref, selt_ref, o_ref):
    a = _dot_split3(part_ref[...], sel_ref[...])
    w = gate_ref[...] * (0.5 * a * (1.0 + lax.erf(a * (2.0 ** -0.5))))
    o_ref[...] = _dot_split3(w, selt_ref[...])


def _gelu_gate(part, gates, tm=512):
    n, wide = part.shape
    nsel = gates.shape[1]
    lanes = wide // nsel
    sel = (jnp.arange(wide)[:, None] // lanes == jnp.arange(nsel)[None, :]).astype(BF16)
    const = lambda i: (0, 0)
    return pl.pallas_call(
        _gelu_gate_kernel,
        grid=(n // tm,),
        in_specs=[pl.BlockSpec((tm, wide), lambda i: (i, 0)), pl.BlockSpec((tm, nsel), lambda i: (i, 0)),
                  pl.BlockSpec((wide, nsel), const), pl.BlockSpec((nsel, wide), const)],
        out_specs=pl.BlockSpec((tm, wide), lambda i: (i, 0)),
        out_shape=jax.ShapeDtypeStruct((n, wide), F32),
        compiler_params=_cparams("parallel"),
        name="gelu_gate",
    )(part, gates, sel, sel.T)


def _final_kernel(h_ref, ffn_ref, p_ref, wg_ref, wp_ref, g_ref, b_ref, o_ref):
    h = h_ref[...]
    ple = _sigmoid(_dot(h.astype(BF16), wg_ref[...])) * _dot(p_ref[...].astype(BF16), wp_ref[...])
    o_ref[...] = _layer_norm(ALPHA * h + ffn_ref[...] + ple, g_ref[...], b_ref[...])


def _final(h1, tok0, ffn, p4, bi, wg, wp, g, b, tm=512):
    n, d = ffn.shape
    off = tok0 // tm
    const = lambda i: (0, 0)
    rowblk = lambda w: pl.BlockSpec((tm, w), lambda i: (i, 0))
    vec = pl.BlockSpec((1, d), const)
    return pl.pallas_call(
        _final_kernel,
        grid=(n // tm,),
        in_specs=[pl.BlockSpec((tm, d), lambda i: (off + i, 0)), rowblk(d),
                  pl.BlockSpec((None, None, tm, p4.shape[3]), lambda i: (0, bi, off + i, 0)),
                  pl.BlockSpec(wg.shape, const), pl.BlockSpec(wp.shape, const), vec, vec],
        out_specs=rowblk(d),
        out_shape=jax.ShapeDtypeStruct((n, d), F32),
        compiler_params=_cparams("parallel"),
        name="ple_final_ln",
    )(h1, ffn, p4, wg, wp, g, b)


SC_CORES = 2
SC_SUBCORES = 16
SC_LANES = 16
SC_WORKERS = SC_CORES * SC_SUBCORES
PEER_NSEL = PEER_HEADS * PEER_TOPK
SC_ROWS = 16
SC_NBUF = 4
SC_GROUP = 8


def _sc_mesh():
    return plsc.VectorSubcoreMesh(core_axis_name="c", subcore_axis_name="s", num_cores=SC_CORES,
                                  num_subcores=SC_SUBCORES)


def _sc_grouped_pipeline(ngroups, stage_copies, out_copy, tab_hbm, idx_v, buf, sems, begin_group, compute):
    nchunk = PEER_NSEL // SC_ROWS
    nsteps = SC_GROUP * nchunk
    assert nsteps % SC_NBUF == 0 and ngroups % 2 == 0

    def gather(gs, step, slot):
        t = step // nchunk
        c = step % nchunk
        return pltpu.make_async_copy(tab_hbm.at[idx_v.at[gs, t, pl.ds(c * SC_ROWS, SC_ROWS)]], buf.at[slot],
                                     sems.at[slot])

    for cp in stage_copies(0, 0):
        cp.start()
    for cp in stage_copies(0, 0):
        cp.wait()
    for b in range(SC_NBUF - 1):
        gather(0, b, b).start()
    for cp in stage_copies(1, 1):
        cp.start()

    def group_pair(gp, _):
        for gs in range(2):
            g = 2 * gp + gs

            @pl.when(g >= 2)
            def _():
                out_copy(g - 2, gs).wait()

            begin_group(gs)

            def ring_round(q, _):
                for b in range(SC_NBUF):
                    s = q * SC_NBUF + b
                    ahead = s + SC_NBUF - 1
                    slot_a = (b + SC_NBUF - 1) % SC_NBUF

                    @pl.when(ahead < nsteps)
                    def _():
                        gather(gs, ahead, slot_a).start()

                    if b > 0:
                        @pl.when(jnp.logical_and(ahead >= nsteps, g + 1 < ngroups))
                        def _():
                            if b == 1:
                                for cp in stage_copies(g + 1, 1 - gs):
                                    cp.wait()
                            gather(1 - gs, ahead - nsteps, slot_a).start()

                    gather(gs, s, b).wait()
                    compute(gs, s // nchunk, s % nchunk, b)
                return 0

            lax.fori_loop(0, nsteps // SC_NBUF, ring_round, 0)
            out_copy(g, gs).start()

            @pl.when(g + 2 < ngroups)
            def _():
                for cp in stage_copies(g + 2, gs):
                    cp.start()
        return 0

    lax.fori_loop(0, ngroups // 2, group_pair, 0)
    out_copy(ngroups - 2, 0).wait()
    out_copy(ngroups - 1, 1).wait()


def _peer_act_sc(h1, tok0, idx, u_tab):
    n = idx.shape[0]
    d = h1.shape[1]
    per_w = n // SC_WORKERS
    hold = 32

    @functools.partial(
        pl.kernel, mesh=_sc_mesh(),
        out_type=jax.ShapeDtypeStruct((n, PEER_NSEL * SC_LANES), F32),
        scratch_types=[
            pltpu.VMEM((2, SC_GROUP, d), F32),
            pltpu.VMEM((2, SC_GROUP, PEER_NSEL), I32),
            pltpu.VMEM((2, SC_GROUP, PEER_NSEL * SC_LANES), F32),
            pltpu.VMEM((SC_NBUF, SC_ROWS, d), F32),
            pltpu.SemaphoreType.DMA((SC_NBUF,)),
            pltpu.SemaphoreType.DMA((2,)),
            pltpu.SemaphoreType.DMA((2,)),
            pltpu.SemaphoreType.DMA((2,)),
        ],
        name="peer_act_sc",
    )
    def k(h_hbm, idx_hbm, u_hbm, out_hbm, h_v, idx_v, out_v, buf, sems, sem_h, sem_i, sem_o):
        base = (lax.axis_index("s") * SC_CORES + lax.axis_index("c")) * per_w

        def stage_copies(g, gs):
            row0 = base + g * SC_GROUP
            return (pltpu.make_async_copy(h_hbm.at[pl.ds(tok0 + row0, SC_GROUP)], h_v.at[gs], sem_h.at[gs]),
                    pltpu.make_async_copy(idx_hbm.at[pl.ds(row0, SC_GROUP)], idx_v.at[gs], sem_i.at[gs]))

        def out_copy(g, gs):
            return pltpu.make_async_copy(out_v.at[gs], out_hbm.at[pl.ds(base + g * SC_GROUP, SC_GROUP)],
                                         sem_o.at[gs])

        def compute(gs, t, c, slot):
            for jb in range(d // (hold * SC_LANES)):
                col0 = jb * hold * SC_LANES
                hv = [h_v[gs, t, pl.ds(col0 + jj * SC_LANES, SC_LANES)] for jj in range(hold)]

                @plsc.parallel_loop(0, SC_ROWS)
                def _(r):
                    ps = [buf[slot, r, pl.ds(col0 + jj * SC_LANES, SC_LANES)] * hv[jj] for jj in range(hold)]
                    while len(ps) > 1:
                        ps = [ps[i] + ps[i + 1] for i in range(0, len(ps), 2)]
                    dst = out_v.at[gs, t, pl.ds((c * SC_ROWS + r) * SC_LANES, SC_LANES)]
                    if jb == 0:
                        dst[...] = ps[0]
                    else:
                        plsc.addupdate(dst, ps[0])

        _sc_grouped_pipeline(per_w // SC_GROUP, stage_copies, out_copy, u_hbm, idx_v, buf, sems, lambda gs: None,
                             compute)

    return k(h1, idx, u_tab)


def _peer_combine_sc(wexp, idx, v_tab):
    n = idx.shape[0]
    d = v_tab.shape[1]
    per_w = n // SC_WORKERS
    nvec = d // SC_LANES
    ncol = 4

    @functools.partial(
        pl.kernel, mesh=_sc_mesh(),
        out_type=jax.ShapeDtypeStruct((n, d), F32),
        scratch_types=[
            pltpu.VMEM((2, SC_GROUP, PEER_NSEL * SC_LANES), F32),
            pltpu.VMEM((2, SC_GROUP, PEER_NSEL), I32),
            pltpu.VMEM((2, SC_GROUP, d), F32),
            pltpu.VMEM((SC_NBUF, SC_ROWS, d), F32),
            pltpu.SemaphoreType.DMA((SC_NBUF,)),
            pltpu.SemaphoreType.DMA((2,)),
            pltpu.SemaphoreType.DMA((2,)),
            pltpu.SemaphoreType.DMA((2,)),
        ],
        name="peer_combine_sc",
    )
    def k(w_hbm, idx_hbm, v_hbm, out_hbm, w_v, idx_v, o_v, buf, sems, sem_w, sem_i, sem_o):
        base = (lax.axis_index("s") * SC_CORES + lax.axis_index("c")) * per_w

        def stage_copies(g, gs):
            row0 = base + g * SC_GROUP
            return (pltpu.make_async_copy(w_hbm.at[pl.ds(row0, SC_GROUP)], w_v.at[gs], sem_w.at[gs]),
                    pltpu.make_async_copy(idx_hbm.at[pl.ds(row0, SC_GROUP)], idx_v.at[gs], sem_i.at[gs]))

        def out_copy(g, gs):
            return pltpu.make_async_copy(o_v.at[gs], out_hbm.at[pl.ds(base + g * SC_GROUP, SC_GROUP)], sem_o.at[gs])

        def begin_group(gs):
            def zero(i, _):
                o_v[gs, i // nvec, pl.ds((i % nvec) * SC_LANES, SC_LANES)] = jnp.zeros((SC_LANES,), F32)
                return 0

            lax.fori_loop(0, SC_GROUP * nvec, zero, 0)

        def compute(gs, t, c, slot):
            @plsc.parallel_loop(0, nvec // ncol, unroll=2)
            def _(cb):
                col0 = cb * ncol * SC_LANES
                acc = [None] * ncol
                for r in range(SC_ROWS):
                    w = w_v[gs, t, pl.ds((c * SC_ROWS + r) * SC_LANES, SC_LANES)]
                    for kk in range(ncol):
                        x = w * buf[slot, r, pl.ds(col0 + kk * SC_LANES, SC_LANES)]
                        acc[kk] = x if acc[kk] is None else acc[kk] + x
                for kk in range(ncol):
                    plsc.addupdate(o_v.at[gs, t, pl.ds(col0 + kk * SC_LANES, SC_LANES)], acc[kk])

        _sc_grouped_pipeline(per_w // SC_GROUP, stage_copies, out_copy, v_hbm, idx_v, buf, sems, begin_group, compute)

    return k(wexp, idx, v_tab)


def _token_mixer_and_norm(x3, bi, g0, b0, wa, wif, wg, gbias, conv_w, conv_b, wsb, wml, wout, g1, b1):
    s = x3.shape[1]
    n_if = 2 * ML_HEADS
    pa, gif, gate = _inproj(x3, bi, g0, b0, wa, wif, wg)
    pa3 = pa.reshape(1, s, PA_WIDTH)
    o_sb = _sb_attention(pa3)
    qk3 = _conv_silu(pa3, conv_w, conv_b)
    grow = jnp.swapaxes(gif.reshape(1, s, LANES)[:, :, :n_if], 1, 2)
    h_ml = _mlstm(qk3, pa3, grow, gbias)
    return _merge(x3, bi, o_sb.reshape(s, SB_WIDTH), h_ml.reshape(s, ML_WIDTH), pa, gate, g0, b0, wsb, wml, wout,
                  g1, b1)


def _after(value, prev):
    if prev is None:
        return value
    return lax.optimization_barrier((value, prev))[0]


def kernel(x, p, ln0_g, ln0_b, w_in, b_igate, b_fgate, conv_w, conv_b, w_branch_sb, w_branch_ml, w_out, ln1_g, ln1_b, peer_wq, peer_k1, peer_k2, peer_u, peer_v, w_ple_gate, w_ple, ln2_g, ln2_b):
    b, s, d = x.shape
    assert w_in.shape[0] == DEPTH
    i = 0
    row = lambda v: v.reshape(1, -1)
    n_if = 2 * ML_HEADS
    wa = w_in[i][:, :PA_WIDTH].astype(BF16)
    wif = jnp.pad(w_in[i][:, PA_WIDTH:PA_WIDTH + n_if], ((0, 0), (0, LANES - n_if))).astype(BF16)
    wg = w_in[i][:, PA_WIDTH + n_if:].astype(BF16)
    gbias = jnp.concatenate([b_igate[i], b_fgate[i]]).reshape(n_if, 1)
    wsb, wml, wout = w_branch_sb[i].astype(BF16), w_branch_ml[i].astype(BF16), w_out[i].astype(BF16)
    wq, k1, k2 = peer_wq[i].astype(BF16), peer_k1[i].astype(BF16), peer_k2[i].astype(BF16)
    wpg, wp = w_ple_gate[i].astype(BF16), w_ple[i].astype(BF16)

    halves = 2
    sp = s // halves
    h1s, routed, parts, ffns, outs = {}, {}, {}, {}, {}
    last = None

    def mix(e):
        h1s[e] = _token_mixer_and_norm(x, e, _after(row(ln0_g), last), row(ln0_b), wa, wif, wg, gbias, conv_w[i],
                                       row(conv_b[i]), wsb, wml, wout, row(ln1_g[i]), row(ln1_b[i]))
        return h1s[e]

    def route(pc):
        e, hf = divmod(pc, halves)
        idx_t, gate_t = _peer_route(_after(h1s[e], last), hf * sp, sp, wq, k1, k2)
        idx = idx_t.T
        routed[pc] = (idx, gate_t.T)
        parts[pc] = _peer_act_sc(h1s[e], hf * sp, _after(idx, ffns.get(pc - 2)), peer_u[i])
        return idx_t

    def gelu(pc):
        e, hf = divmod(pc, halves)
        idx, gates = routed[pc]
        wexp = _gelu_gate(parts[pc], _after(gates, last))
        ffns[pc] = _peer_combine_sc(wexp, idx, peer_v[i])
        outs[pc] = _final(h1s[e], hf * sp, ffns[pc], p, e, wpg, wp, row(ln2_g[i]), row(ln2_b[i]))
        return wexp

    order = [(mix, 0), (route, 0), (route, 1)]
    for e in range(1, b):
        order += [(mix, e), (gelu, 2 * e - 2), (route, 2 * e), (gelu, 2 * e - 1), (route, 2 * e + 1)]
    order += [(gelu, 2 * b - 2), (gelu, 2 * b - 1)]
    for stage, arg in order:
        last = stage(arg)
    return jnp.concatenate([outs[pc] for pc in range(halves * b)]).reshape(b, s, d)
```

```python
import functools

import jax
import jax.numpy as jnp
from jax import lax
from jax.experimental import pallas as pl
from jax.experimental.pallas import tpu as pltpu
from jax.experimental.pallas import tpu_sc as plsc

F32 = jnp.float32
BF16 = jnp.bfloat16
I32 = jnp.int32

D_MODEL = 1024
SB_HEADS = 8
SB_HEAD_DIM = 64
SB_WIDTH = SB_HEADS * SB_HEAD_DIM
ML_HEADS = 4
ML_HEAD_DIM = 128
ML_WIDTH = ML_HEADS * ML_HEAD_DIM
ML_CHUNK = 128
CONV_WIDTH = 4
PEER_HEADS = 8
PEER_KEYS = 128
PEER_QDIM = 256
PEER_HALF = PEER_QDIM // 2
PEER_TOPK = 16
PEER_BLOCK = 128
DEPTH = 1
ALPHA = (2.0 * DEPTH) ** 0.25
LN_EPS = 1e-5

LANES = 128
VMEM_LIMIT = 56 * 1024 * 1024
SB_SKIP_LOG = -104.0

PA_WIDTH = 3 * SB_WIDTH + 4 * ML_WIDTH


def _cparams(*sem):
    return pltpu.CompilerParams(dimension_semantics=sem, vmem_limit_bytes=VMEM_LIMIT)


def _layer_norm(x, g, b):
    mu = jnp.mean(x, axis=-1, keepdims=True)
    xc = x - mu
    var = jnp.mean(xc * xc, axis=-1, keepdims=True)
    return xc * lax.rsqrt(var + LN_EPS) * g + b


def _log_sigmoid(z):
    return jnp.minimum(z, 0.0) - jnp.log1p(jnp.exp(-jnp.abs(z)))


def _sigmoid(z):
    return 1.0 / (1.0 + jnp.exp(-z))


def _dot(a, b):
    return jnp.dot(a, b, preferred_element_type=F32)


def _dot_nt(a, b):
    return lax.dot_general(a, b, (((1,), (1,)), ((), ())), preferred_element_type=F32)


def _dot_tn(a, b):
    return lax.dot_general(a, b, (((0,), (0,)), ((), ())), preferred_element_type=F32)


def _inproj_kernel(x_ref, g_ref, b_ref, wa_ref, wif_ref, wg_ref, oa_ref, oif_ref, og_ref, *, cw):
    h = _layer_norm(x_ref[...], g_ref[...], b_ref[...]).astype(BF16)
    for j in range(0, wa_ref.shape[1], cw):
        oa_ref[:, j:j + cw] = _dot(h, wa_ref[:, j:j + cw]).astype(BF16)
    oif_ref[...] = _dot(h, wif_ref[...])
    for j in range(0, wg_ref.shape[1], cw):
        og_ref[:, j:j + cw] = _dot(h, wg_ref[:, j:j + cw]).astype(BF16)


def _inproj(x3, bi, g, b, wa, wif, wg, tm=512):
    _, n, d = x3.shape
    const = lambda i: (0, 0)
    return pl.pallas_call(
        functools.partial(_inproj_kernel, cw=512),
        grid=(n // tm,),
        in_specs=[
            pl.BlockSpec((None, tm, d), lambda i: (bi, i, 0)),
            pl.BlockSpec((1, d), const),
            pl.BlockSpec((1, d), const),
            pl.BlockSpec(wa.shape, const),
            pl.BlockSpec(wif.shape, const),
            pl.BlockSpec(wg.shape, const),
        ],
        out_specs=[
            pl.BlockSpec((tm, wa.shape[1]), lambda i: (i, 0)),
            pl.BlockSpec((tm, wif.shape[1]), lambda i: (i, 0)),
            pl.BlockSpec((tm, wg.shape[1]), lambda i: (i, 0)),
        ],
        out_shape=[
            jax.ShapeDtypeStruct((n, wa.shape[1]), BF16),
            jax.ShapeDtypeStruct((n, wif.shape[1]), F32),
            jax.ShapeDtypeStruct((n, wg.shape[1]), BF16),
        ],
        compiler_params=_cparams("parallel"),
        name="ln_inproj",
    )(x3, g, b, wa, wif, wg)


def _sb_kernel(q_ref, k_ref, v_ref, o_ref, *, tq):
    qi = pl.program_id(2)
    q = q_ref[0]
    lane = lax.broadcasted_iota(I32, (1, LANES), 1)
    row = lax.broadcasted_iota(I32, (tq, tq), 0)
    col = lax.broadcasted_iota(I32, (tq, tq), 1)
    later = (row > col).astype(BF16)
    scale = SB_HEAD_DIM ** -0.5
    first = lane < SB_HEAD_DIM
    zero = jnp.zeros_like(q)
    q2 = jnp.concatenate([jnp.where(first, q, zero), jnp.where(first, zero, q)], axis=0)
    q2 = (q2.astype(F32) * scale).astype(BF16)
    row2 = lax.broadcasted_iota(I32, (2 * tq, tq), 0)
    causal = lax.broadcasted_iota(I32, (2 * tq, tq), 1) < jnp.where(row2 >= tq, row2 - tq, row2)

    def tile(qh, kb, carry, masked):
        off = pl.multiple_of(kb * tq, tq)
        k_blk = k_ref[0, pl.ds(off, tq), :]
        v_blk = v_ref[0, pl.ds(off, tq), :]
        z = _dot_nt(qh, k_blk)
        lb = _log_sigmoid(z)
        lom = lb - z
        if masked:
            lom = jnp.where(causal, lom, 0.0)
        hi = lom.astype(BF16)
        lo = (lom - hi.astype(F32)).astype(BF16)
        tail = _dot(hi, later) + _dot(lo, later) + carry
        w = jnp.exp(lb + tail)
        if masked:
            w = jnp.where(causal, w, 0.0)
        contrib = _dot(w.astype(BF16), v_blk)
        return contrib, carry + jnp.sum(lom, axis=1, keepdims=True)

    acc, carry = tile(q2, qi, jnp.zeros((2 * tq, 1), F32), True)

    def cond(st):
        kb, _, _, cmax = st
        return jnp.logical_and(kb >= 0, cmax > SB_SKIP_LOG)

    def body(st):
        kb, acc, carry, _ = st
        contrib, carry = tile(q2, kb, carry, False)
        return kb - 1, acc + contrib, carry, jnp.max(carry)

    _, acc, _, _ = lax.while_loop(cond, body, (qi - 1, acc, carry, jnp.max(carry)))
    o_ref[0] = jnp.where(first, acc[:tq], acc[tq:]).astype(o_ref.dtype)


def _sb_attention(pa3, tq=128):
    b, s, _ = pa3.shape
    npair = SB_WIDTH // LANES
    return pl.pallas_call(
        functools.partial(_sb_kernel, tq=tq),
        grid=(b, npair, s // tq),
        in_specs=[
            pl.BlockSpec((1, tq, LANES), lambda bi, hp, qi: (bi, qi, hp)),
            pl.BlockSpec((1, s, LANES), lambda bi, hp, qi: (bi, 0, npair + hp)),
            pl.BlockSpec((1, s, LANES), lambda bi, hp, qi: (bi, 0, 2 * npair + hp)),
        ],
        out_specs=pl.BlockSpec((1, tq, LANES), lambda bi, hp, qi: (bi, qi, hp)),
        out_shape=jax.ShapeDtypeStruct((b, s, SB_WIDTH), BF16),
        compiler_params=_cparams("parallel", "parallel", "arbitrary"),
        name="sb_attention",
    )(pa3, pa3, pa3)


CONV_HALO = 16


def _conv_kernel(x_ref, prev_ref, w_ref, b_ref, o_ref, buf_ref, *, ts):
    si = pl.program_id(1)
    is_k = pl.program_id(2)
    prev = prev_ref[0].astype(F32)
    buf_ref[0:CONV_HALO, :] = jnp.where(si == 0, 0.0, prev)
    buf_ref[CONV_HALO:, :] = x_ref[0].astype(F32)
    y = b_ref[...] + w_ref[0:1, :] * buf_ref[CONV_HALO:, :]
    for j in range(1, CONV_WIDTH):
        y = y + w_ref[j:j + 1, :] * buf_ref[CONV_HALO - j:CONV_HALO - j + ts, :]
    y = y * _sigmoid(y)
    y = y * jnp.where(is_k == 1, ML_HEAD_DIM ** -0.5, 1.0)
    o_ref[0] = y.astype(o_ref.dtype)


def _conv_silu(pa3, conv_w, conv_b, ts=1024):
    b, s, _ = pa3.shape
    ts = min(ts, s)
    cb = ML_WIDTH
    base = 3 * SB_WIDTH // cb
    hb = ts // CONV_HALO
    return pl.pallas_call(
        functools.partial(_conv_kernel, ts=ts),
        grid=(b, s // ts, 2),
        in_specs=[
            pl.BlockSpec((1, ts, cb), lambda bi, si, j: (bi, si, base + j)),
            pl.BlockSpec((1, CONV_HALO, cb), lambda bi, si, j: (bi, jnp.maximum(si * hb - 1, 0), base + j)),
            pl.BlockSpec((CONV_WIDTH, cb), lambda bi, si, j: (0, j)),
            pl.BlockSpec((1, cb), lambda bi, si, j: (0, j)),
        ],
        out_specs=pl.BlockSpec((1, ts, cb), lambda bi, si, j: (bi, si, j)),
        out_shape=jax.ShapeDtypeStruct((b, s, 2 * ML_WIDTH), BF16),
        scratch_shapes=[pltpu.VMEM((ts + CONV_HALO, cb), F32)],
        compiler_params=_cparams("parallel", "parallel", "parallel"),
        name="conv_silu",
    )(pa3, pa3, conv_w, conv_b)


def _mlstm_kernel(q_ref, k_ref, v_ref, g_ref, gb_ref, o_ref):
    L = ML_CHUNK
    head = pl.program_id(1)
    nc = q_ref.shape[1] // L
    r = lax.broadcasted_iota(I32, (L, L), 0)
    c = lax.broadcasted_iota(I32, (L, L), 1)
    eye = r == c
    sub = lax.broadcasted_iota(I32, (2 * ML_HEADS, L), 0)

    def to_col(row):
        return jnp.sum(jnp.where(eye, jnp.broadcast_to(row, (L, L)), 0.0), axis=1, keepdims=True)

    def chunk(ci, st):
        c_st, n_st, m_st = st
        off = pl.multiple_of(ci * L, L)
        q = q_ref[0, pl.ds(off, L), :]
        k = k_ref[0, pl.ds(off, L), :]
        v = v_ref[0, pl.ds(off, L), :]
        g = g_ref[0, :, pl.ds(off, L)] + gb_ref[...]
        li_row = jnp.sum(jnp.where(sub == head, g, 0.0), axis=0, keepdims=True)
        lf_row = _log_sigmoid(jnp.sum(jnp.where(sub == head + ML_HEADS, g, 0.0), axis=0, keepdims=True))
        lf_b = jnp.broadcast_to(lf_row, (L, L))
        bcum_col = jnp.sum(jnp.where(c <= r, lf_b, 0.0), axis=1, keepdims=True)
        lf_col = to_col(lf_row)
        li_col = to_col(li_row)
        bcum_row = jnp.sum(jnp.where(r <= c, jnp.broadcast_to(lf_col, (L, L)), 0.0), axis=0, keepdims=True)
        b_last = jnp.sum(lf_row, axis=1, keepdims=True)

        d_log = jnp.where(c <= r, bcum_col - bcum_row + li_row, -jnp.inf)
        inter = bcum_col + m_st
        m_t = jnp.maximum(inter, jnp.max(d_log, axis=1, keepdims=True))
        a_t = jnp.exp(inter - m_t)
        s_w = _dot_nt(q, k) * jnp.exp(d_log - m_t)
        num = a_t * _dot_nt(q, c_st.astype(BF16)) + _dot(s_w.astype(BF16), v)
        den = a_t * jnp.sum(q.astype(F32) * n_st, axis=1, keepdims=True) + jnp.sum(s_w, axis=1, keepdims=True)
        o_ref[0, pl.ds(off, L), :] = (num / jnp.maximum(jnp.abs(den), jnp.exp(-m_t))).astype(o_ref.dtype)

        w_end = b_last - bcum_col + li_col
        m_loc = jnp.max(w_end, axis=0, keepdims=True)
        e_end = jnp.exp(w_end - m_loc)
        c_loc = _dot_tn((e_end * v.astype(F32)).astype(BF16), k)
        n_loc = jnp.sum(e_end * k.astype(F32), axis=0, keepdims=True)
        m_new = jnp.maximum(b_last + m_st, m_loc)
        a = jnp.exp(b_last + m_st - m_new)
        gg = jnp.exp(m_loc - m_new)
        return a * c_st + gg * c_loc, a * n_st + gg * n_loc, m_new

    init = (jnp.zeros((ML_HEAD_DIM, ML_HEAD_DIM), F32), jnp.zeros((1, ML_HEAD_DIM), F32), jnp.zeros((1, 1), F32))
    lax.fori_loop(0, nc, chunk, init)


def _mlstm(qk3, pa3, grow, gbias):
    b, s, _ = pa3.shape
    vbase = (3 * SB_WIDTH + 2 * ML_WIDTH) // ML_HEAD_DIM
    return pl.pallas_call(
        _mlstm_kernel,
        grid=(b, ML_HEADS),
        in_specs=[
            pl.BlockSpec((1, s, ML_HEAD_DIM), lambda bi, h: (bi, 0, h)),
            pl.BlockSpec((1, s, ML_HEAD_DIM), lambda bi, h: (bi, 0, ML_HEADS + h)),
            pl.BlockSpec((1, s, ML_HEAD_DIM), lambda bi, h: (bi, 0, vbase + h)),
            pl.BlockSpec((1, 2 * ML_HEADS, s), lambda bi, h: (bi, 0, 0)),
            pl.BlockSpec((2 * ML_HEADS, 1), lambda bi, h: (0, 0)),
        ],
        out_specs=pl.BlockSpec((1, s, ML_HEAD_DIM), lambda bi, h: (bi, 0, h)),
        out_shape=jax.ShapeDtypeStruct((b, s, ML_WIDTH), BF16),
        compiler_params=_cparams("parallel", "parallel"),
        name="mlstm",
    )(qk3, qk3, pa3, grow, gbias)


def _merge_kernel(x_ref, osb_ref, hml_ref, mlo_ref, gate_ref, g0_ref, b0_ref, wsb_ref, wml_ref, wout_ref,
                  g1_ref, b1_ref, o_ref):
    h0 = _layer_norm(x_ref[...], g0_ref[...], b0_ref[...])
    o_ml = (_sigmoid(mlo_ref[...].astype(F32)) * hml_ref[...].astype(F32)).astype(BF16)
    y = _sigmoid(gate_ref[:, :D_MODEL].astype(F32)) * _dot(osb_ref[...], wsb_ref[...])
    y = y + _sigmoid(gate_ref[:, D_MODEL:].astype(F32)) * _dot(o_ml, wml_ref[...])
    mix = _dot(y.astype(BF16), wout_ref[...])
    o_ref[...] = _layer_norm(ALPHA * h0 + mix, g1_ref[...], b1_ref[...])


def _merge(x3, bi, osb, hml, pa, gate, g0, b0, wsb, wml, wout, g1, b1, tm=512):
    _, n, d = x3.shape
    const = lambda i: (0, 0)
    rowblk = lambda w: pl.BlockSpec((tm, w), lambda i: (i, 0))
    vec = pl.BlockSpec((1, d), const)
    mlo_blk = (3 * SB_WIDTH + 3 * ML_WIDTH) // ML_WIDTH
    return pl.pallas_call(
        _merge_kernel,
        grid=(n // tm,),
        in_specs=[
            pl.BlockSpec((None, tm, d), lambda i: (bi, i, 0)), rowblk(SB_WIDTH), rowblk(ML_WIDTH),
            pl.BlockSpec((tm, ML_WIDTH), lambda i: (i, mlo_blk)),
            rowblk(2 * d), vec, vec,
            pl.BlockSpec(wsb.shape, const), pl.BlockSpec(wml.shape, const), pl.BlockSpec(wout.shape, const),
            vec, vec,
        ],
        out_specs=rowblk(d),
        out_shape=jax.ShapeDtypeStruct((n, d), F32),
        compiler_params=_cparams("parallel"),
        name="merge_outproj_ln",
    )(x3, osb, hml, pa, gate, g0, b0, wsb, wml, wout, g1, b1)


def _topk_rows(s, k):
    n = s.shape[0]
    iota = lax.broadcasted_iota(I32, s.shape, 0)
    vals, idxs = [], []
    for _ in range(k):
        m = jnp.max(s, axis=0, keepdims=True)
        am = jnp.min(jnp.where(s == m, iota, n), axis=0, keepdims=True)
        vals.append(m)
        idxs.append(am)
        s = jnp.where(iota == am, -jnp.inf, s)
    return jnp.concatenate(vals, axis=0), jnp.concatenate(idxs, axis=0)


def _select_rows(sel, table):
    out = jnp.zeros(sel.shape, table.dtype)
    for r_ in range(table.shape[0]):
        out = jnp.where(sel == r_, table[r_:r_ + 1, :], out)
    return out


def _peer_route_kernel(h_ref, wq_ref, k1_ref, k2_ref, idx_ref, gate_ref, q_scr, *, tt):
    q_scr[...] = _dot(h_ref[...].astype(BF16), wq_ref[...]).astype(BF16)
    nsub = h_ref.shape[0] // tt

    def one(it, _):
        head = it % PEER_HEADS
        sub = it // PEER_HEADS
        roff = pl.multiple_of(sub * tt, tt)
        coff = pl.multiple_of(head * PEER_QDIM, PEER_QDIM)
        q1 = q_scr[pl.ds(roff, tt), pl.ds(coff, PEER_HALF)]
        q2 = q_scr[pl.ds(roff, tt), pl.ds(coff + PEER_HALF, PEER_HALF)]
        v1, i1 = _topk_rows(_dot_nt(k1_ref[...], q1), PEER_TOPK)
        v2, i2 = _topk_rows(_dot_nt(k2_ref[...], q2), PEER_TOPK)
        cand = jnp.concatenate([v1[i:i + 1, :] + v2 for i in range(PEER_TOPK)], axis=0)
        top_s, pos = _topk_rows(cand, PEER_TOPK)
        e1 = _select_rows(pos // PEER_TOPK, i1)
        e2 = _select_rows(pos % PEER_TOPK, i2)
        ex = jnp.exp(top_s - top_s[0:1, :])
        gates = ex / jnp.sum(ex, axis=0, keepdims=True)
        hoff = pl.multiple_of(head * PEER_TOPK, PEER_TOPK)
        idx_ref[pl.ds(hoff, PEER_TOPK), pl.ds(roff, tt)] = e1 * PEER_KEYS + e2
        gate_ref[pl.ds(hoff, PEER_TOPK), pl.ds(roff, tt)] = gates
        return 0

    lax.fori_loop(0, nsub * PEER_HEADS, one, 0)


def _peer_route(h1, tok0, n, wq, k1, k2, tm=256, tt=128):
    d = h1.shape[1]
    off = tok0 // tm
    const = lambda i: (0, 0)
    nsel = PEER_HEADS * PEER_TOPK
    return pl.pallas_call(
        functools.partial(_peer_route_kernel, tt=tt),
        grid=(n // tm,),
        in_specs=[
            pl.BlockSpec((tm, d), lambda i: (off + i, 0)),
            pl.BlockSpec(wq.shape, const),
            pl.BlockSpec(k1.shape, const),
            pl.BlockSpec(k2.shape, const),
        ],
        out_specs=[pl.BlockSpec((nsel, tm), lambda i: (0, i)), pl.BlockSpec((nsel, tm), lambda i: (0, i))],
        out_shape=[jax.ShapeDtypeStruct((nsel, n), I32), jax.ShapeDtypeStruct((nsel, n), F32)],
        scratch_shapes=[pltpu.VMEM((tm, PEER_HEADS * PEER_QDIM), BF16)],
        compiler_params=_cparams("parallel"),
        name="peer_route",
    )(h1, wq, k1, k2)


def _dot_split3(x, sel):
    hi = x.astype(BF16)
    r1 = x - hi.astype(F32)
    mid = r1.astype(BF16)
    lo = (r1 - mid.astype(F32)).astype(BF16)
    return _dot(hi, sel) + _dot(mid, sel) + _dot(lo, sel)


def _gelu_gate_kernel(part_ref, gate_ref, sel_ref, selt_ref, o_ref):
    a = _dot_split3(part_ref[...], sel_ref[...])
    w = gate_ref[...] * (0.5 * a * (1.0 + lax.erf(a * (2.0 ** -0.5))))
    o_ref[...] = _dot_split3(w, selt_ref[...])


def _gelu_gate(part, gates, tm=512):
    n, wide = part.shape
    nsel = gates.shape[1]
    lanes = wide // nsel
    sel = (jnp.arange(wide)[:, None] // lanes == jnp.arange(nsel)[None, :]).astype(BF16)
    const = lambda i: (0, 0)
    return pl.pallas_call(
        _gelu_gate_kernel,
        grid=(n // tm,),
        in_specs=[pl.BlockSpec((tm, wide), lambda i: (i, 0)), pl.BlockSpec((tm, nsel), lambda i: (i, 0)),
                  pl.BlockSpec((wide, nsel), const), pl.BlockSpec((nsel, wide), const)],
        out_specs=pl.BlockSpec((tm, wide), lambda i: (i, 0)),
        out_shape=jax.ShapeDtypeStruct((n, wide), F32),
        compiler_params=_cparams("parallel"),
        name="gelu_gate",
    )(part, gates, sel, sel.T)


def _final_kernel(h_ref, ffn_ref, p_ref, wg_ref, wp_ref, g_ref, b_ref, o_ref):
    h = h_ref[...]
    ple = _sigmoid(_dot(h.astype(BF16), wg_ref[...])) * _dot(p_ref[...].astype(BF16), wp_ref[...])
    o_ref[...] = _layer_norm(ALPHA * h + ffn_ref[...] + ple, g_ref[...], b_ref[...])


def _final(h1, tok0, ffn, p4, bi, wg, wp, g, b, tm=512):
    n, d = ffn.shape
    off = tok0 // tm
    const = lambda i: (0, 0)
    rowblk = lambda w: pl.BlockSpec((tm, w), lambda i: (i, 0))
    vec = pl.BlockSpec((1, d), const)
    return pl.pallas_call(
        _final_kernel,
        grid=(n // tm,),
        in_specs=[pl.BlockSpec((tm, d), lambda i: (off + i, 0)), rowblk(d),
                  pl.BlockSpec((None, None, tm, p4.shape[3]), lambda i: (0, bi, off + i, 0)),
                  pl.BlockSpec(wg.shape, const), pl.BlockSpec(wp.shape, const), vec, vec],
        out_specs=rowblk(d),
        out_shape=jax.ShapeDtypeStruct((n, d), F32),
        compiler_params=_cparams("parallel"),
        name="ple_final_ln",
    )(h1, ffn, p4, wg, wp, g, b)


SC_CORES = 2
SC_SUBCORES = 16
SC_LANES = 16
SC_WORKERS = SC_CORES * SC_SUBCORES
PEER_NSEL = PEER_HEADS * PEER_TOPK
SC_ROWS = 16
SC_NBUF = 4
SC_GROUP = 8


def _sc_mesh():
    return plsc.VectorSubcoreMesh(core_axis_name="c", subcore_axis_name="s", num_cores=SC_CORES,
                                  num_subcores=SC_SUBCORES)


def _sc_grouped_pipeline(ngroups, stage_copies, out_copy, tab_hbm, idx_v, buf, sems, begin_group, compute):
    nchunk = PEER_NSEL // SC_ROWS
    nsteps = SC_GROUP * nchunk
    assert nsteps % SC_NBUF == 0 and ngroups % 2 == 0

    def gather(gs, step, slot):
        t = step // nchunk
        c = step % nchunk
        return pltpu.make_async_copy(tab_hbm.at[idx_v.at[gs, t, pl.ds(c * SC_ROWS, SC_ROWS)]], buf.at[slot],
                                     sems.at[slot])

    for cp in stage_copies(0, 0):
        cp.start()
    for cp in stage_copies(0, 0):
        cp.wait()
    for b in range(SC_NBUF - 1):
        gather(0, b, b).start()
    for cp in stage_copies(1, 1):
        cp.start()

    def group_pair(gp, _):
        for gs in range(2):
            g = 2 * gp + gs

            @pl.when(g >= 2)
            def _():
                out_copy(g - 2, gs).wait()

            begin_group(gs)

            def ring_round(q, _):
                for b in range(SC_NBUF):
                    s = q * SC_NBUF + b
                    ahead = s + SC_NBUF - 1
                    slot_a = (b + SC_NBUF - 1) % SC_NBUF

                    @pl.when(ahead < nsteps)
                    def _():
                        gather(gs, ahead, slot_a).start()

                    if b > 0:
                        @pl.when(jnp.logical_and(ahead >= nsteps, g + 1 < ngroups))
                        def _():
                            if b == 1:
                                for cp in stage_copies(g + 1, 1 - gs):
                                    cp.wait()
                            gather(1 - gs, ahead - nsteps, slot_a).start()

                    gather(gs, s, b).wait()
                    compute(gs, s // nchunk, s % nchunk, b)
                return 0

            lax.fori_loop(0, nsteps // SC_NBUF, ring_round, 0)
            out_copy(g, gs).start()

            @pl.when(g + 2 < ngroups)
            def _():
                for cp in stage_copies(g + 2, gs):
                    cp.start()
        return 0

    lax.fori_loop(0, ngroups // 2, group_pair, 0)
    out_copy(ngroups - 2, 0).wait()
    out_copy(ngroups - 1, 1).wait()


def _peer_act_sc(h1, tok0, idx, u_tab):
    n = idx.shape[0]
    d = h1.shape[1]
    per_w = n // SC_WORKERS
    hold = 32

    @functools.partial(
        pl.kernel, mesh=_sc_mesh(),
        out_type=jax.ShapeDtypeStruct((n, PEER_NSEL * SC_LANES), F32),
        scratch_types=[
            pltpu.VMEM((2, SC_GROUP, d), F32),
            pltpu.VMEM((2, SC_GROUP, PEER_NSEL), I32),
            pltpu.VMEM((2, SC_GROUP, PEER_NSEL * SC_LANES), F32),
            pltpu.VMEM((SC_NBUF, SC_ROWS, d), F32),
            pltpu.SemaphoreType.DMA((SC_NBUF,)),
            pltpu.SemaphoreType.DMA((2,)),
            pltpu.SemaphoreType.DMA((2,)),
            pltpu.SemaphoreType.DMA((2,)),
        ],
        name="peer_act_sc",
    )
    def k(h_hbm, idx_hbm, u_hbm, out_hbm, h_v, idx_v, out_v, buf, sems, sem_h, sem_i, sem_o):
        base = (lax.axis_index("s") * SC_CORES + lax.axis_index("c")) * per_w

        def stage_copies(g, gs):
            row0 = base + g * SC_GROUP
            return (pltpu.make_async_copy(h_hbm.at[pl.ds(tok0 + row0, SC_GROUP)], h_v.at[gs], sem_h.at[gs]),
                    pltpu.make_async_copy(idx_hbm.at[pl.ds(row0, SC_GROUP)], idx_v.at[gs], sem_i.at[gs]))

        def out_copy(g, gs):
            return pltpu.make_async_copy(out_v.at[gs], out_hbm.at[pl.ds(base + g * SC_GROUP, SC_GROUP)],
                                         sem_o.at[gs])

        def compute(gs, t, c, slot):
            for jb in range(d // (hold * SC_LANES)):
                col0 = jb * hold * SC_LANES
                hv = [h_v[gs, t, pl.ds(col0 + jj * SC_LANES, SC_LANES)] for jj in range(hold)]

                @plsc.parallel_loop(0, SC_ROWS)
                def _(r):
                    ps = [buf[slot, r, pl.ds(col0 + jj * SC_LANES, SC_LANES)] * hv[jj] for jj in range(hold)]
                    while len(ps) > 1:
                        ps = [ps[i] + ps[i + 1] for i in range(0, len(ps), 2)]
                    dst = out_v.at[gs, t, pl.ds((c * SC_ROWS + r) * SC_LANES, SC_LANES)]
                    if jb == 0:
                        dst[...] = ps[0]
                    else:
                        plsc.addupdate(dst, ps[0])

        _sc_grouped_pipeline(per_w // SC_GROUP, stage_copies, out_copy, u_hbm, idx_v, buf, sems, lambda gs: None,
                             compute)

    return k(h1, idx, u_tab)


def _peer_combine_sc(wexp, idx, v_tab):
    n = idx.shape[0]
    d = v_tab.shape[1]
    per_w = n // SC_WORKERS
    nvec = d // SC_LANES
    ncol = 4

    @functools.partial(
        pl.kernel, mesh=_sc_mesh(),
        out_type=jax.ShapeDtypeStruct((n, d), F32),
        scratch_types=[
            pltpu.VMEM((2, SC_GROUP, PEER_NSEL * SC_LANES), F32),
            pltpu.VMEM((2, SC_GROUP, PEER_NSEL), I32),
            pltpu.VMEM((2, SC_GROUP, d), F32),
            pltpu.VMEM((SC_NBUF, SC_ROWS, d), F32),
            pltpu.SemaphoreType.DMA((SC_NBUF,)),
            pltpu.SemaphoreType.DMA((2,)),
            pltpu.SemaphoreType.DMA((2,)),
            pltpu.SemaphoreType.DMA((2,)),
        ],
        name="peer_combine_sc",
    )
    def k(w_hbm, idx_hbm, v_hbm, out_hbm, w_v, idx_v, o_v, buf, sems, sem_w, sem_i, sem_o):
        base = (lax.axis_index("s") * SC_CORES + lax.axis_index("c")) * per_w

        def stage_copies(g, gs):
            row0 = base + g * SC_GROUP
            return (pltpu.make_async_copy(w_hbm.at[pl.ds(row0, SC_GROUP)], w_v.at[gs], sem_w.at[gs]),
                    pltpu.make_async_copy(idx_hbm.at[pl.ds(row0, SC_GROUP)], idx_v.at[gs], sem_i.at[gs]))

        def out_copy(g, gs):
            return pltpu.make_async_copy(o_v.at[gs], out_hbm.at[pl.ds(base + g * SC_GROUP, SC_GROUP)], sem_o.at[gs])

        def begin_group(gs):
            def zero(i, _):
                o_v[gs, i // nvec, pl.ds((i % nvec) * SC_LANES, SC_LANES)] = jnp.zeros((SC_LANES,), F32)
                return 0

            lax.fori_loop(0, SC_GROUP * nvec, zero, 0)

        def compute(gs, t, c, slot):
            ws = [w_v[gs, t, pl.ds((c * SC_ROWS + r) * SC_LANES, SC_LANES)] for r in range(SC_ROWS)]

            @plsc.parallel_loop(0, nvec // ncol)
            def _(cb):
                col0 = cb * ncol * SC_LANES
                acc = [None] * ncol
                for r in range(SC_ROWS):
                    for kk in range(ncol):
                        x = ws[r] * buf[slot, r, pl.ds(col0 + kk * SC_LANES, SC_LANES)]
                        acc[kk] = x if acc[kk] is None else acc[kk] + x
                for kk in range(ncol):
                    plsc.addupdate(o_v.at[gs, t, pl.ds(col0 + kk * SC_LANES, SC_LANES)], acc[kk])

        _sc_grouped_pipeline(per_w // SC_GROUP, stage_copies, out_copy, v_hbm, idx_v, buf, sems, begin_group, compute)

    return k(wexp, idx, v_tab)


def _token_mixer_and_norm(x3, bi, g0, b0, wa, wif, wg, gbias, conv_w, conv_b, wsb, wml, wout, g1, b1):
    s = x3.shape[1]
    n_if = 2 * ML_HEADS
    pa, gif, gate = _inproj(x3, bi, g0, b0, wa, wif, wg)
    pa3 = pa.reshape(1, s, PA_WIDTH)
    o_sb = _sb_attention(pa3)
    qk3 = _conv_silu(pa3, conv_w, conv_b)
    grow = jnp.swapaxes(gif.reshape(1, s, LANES)[:, :, :n_if], 1, 2)
    h_ml = _mlstm(qk3, pa3, grow, gbias)
    return _merge(x3, bi, o_sb.reshape(s, SB_WIDTH), h_ml.reshape(s, ML_WIDTH), pa, gate, g0, b0, wsb, wml, wout,
                  g1, b1)


def _after(value, prev):
    if prev is None:
        return value
    return lax.optimization_barrier((value, prev))[0]


def kernel(x, p, ln0_g, ln0_b, w_in, b_igate, b_fgate, conv_w, conv_b, w_branch_sb, w_branch_ml, w_out, ln1_g, ln1_b, peer_wq, peer_k1, peer_k2, peer_u, peer_v, w_ple_gate, w_ple, ln2_g, ln2_b):
    b, s, d = x.shape
    assert w_in.shape[0] == DEPTH
    i = 0
    row = lambda v: v.reshape(1, -1)
    n_if = 2 * ML_HEADS
    wa = w_in[i][:, :PA_WIDTH].astype(BF16)
    wif = jnp.pad(w_in[i][:, PA_WIDTH:PA_WIDTH + n_if], ((0, 0), (0, LANES - n_if))).astype(BF16)
    wg = w_in[i][:, PA_WIDTH + n_if:].astype(BF16)
    gbias = jnp.concatenate([b_igate[i], b_fgate[i]]).reshape(n_if, 1)
    wsb, wml, wout = w_branch_sb[i].astype(BF16), w_branch_ml[i].astype(BF16), w_out[i].astype(BF16)
    wq, k1, k2 = peer_wq[i].astype(BF16), peer_k1[i].astype(BF16), peer_k2[i].astype(BF16)
    wpg, wp = w_ple_gate[i].astype(BF16), w_ple[i].astype(BF16)

    halves = 2
    sp = s // halves
    h1s, routed, parts, ffns, outs = {}, {}, {}, {}, {}
    last = None

    def mix(e):
        h1s[e] = _token_mixer_and_norm(x, e, _after(row(ln0_g), last), row(ln0_b), wa, wif, wg, gbias, conv_w[i],
                                       row(conv_b[i]), wsb, wml, wout, row(ln1_g[i]), row(ln1_b[i]))
        return h1s[e]

    def route(pc):
        e, hf = divmod(pc, halves)
        idx_t, gate_t = _peer_route(_after(h1s[e], last), hf * sp, sp, wq, k1, k2)
        idx = idx_t.T
        routed[pc] = (idx, gate_t.T)
        parts[pc] = _peer_act_sc(h1s[e], hf * sp, _after(idx, ffns.get(pc - 2)), peer_u[i])
        return idx_t

    def gelu(pc):
        e, hf = divmod(pc, halves)
        idx, gates = routed[pc]
        wexp = _gelu_gate(parts[pc], _after(gates, last))
        ffns[pc] = _peer_combine_sc(wexp, idx, peer_v[i])
        outs[pc] = _final(h1s[e], hf * sp, ffns[pc], p, e, wpg, wp, row(ln2_g[i]), row(ln2_b[i]))
        return wexp

    order = [(mix, 0), (route, 0), (route, 1)]
    for e in range(1, b):
        order += [(mix, e), (gelu, 2 * e - 2), (route, 2 * e), (gelu, 2 * e - 1), (route, 2 * e + 1)]
    order += [(gelu, 2 * b - 2), (gelu, 2 * b - 1)]
    for stage, arg in order:
        last = stage(arg)
    return jnp.concatenate([outs[pc] for pc in range(halves * b)]).reshape(b, s, d)
```

```python
import functools

import jax
import jax.numpy as jnp
from jax import lax
from jax.experimental import pallas as pl
from jax.experimental.pallas import tpu as pltpu
from jax.experimental.pallas import tpu_sc as plsc

F32 = jnp.float32
BF16 = jnp.bfloat16
I32 = jnp.int32

D_MODEL = 1024
SB_HEADS = 8
SB_HEAD_DIM = 64
SB_WIDTH = SB_HEADS * SB_HEAD_DIM
ML_HEADS = 4
ML_HEAD_DIM = 128
ML_WIDTH = ML_HEADS * ML_HEAD_DIM
ML_CHUNK = 128
CONV_WIDTH = 4
PEER_HEADS = 8
PEER_KEYS = 128
PEER_QDIM = 256
PEER_HALF = PEER_QDIM // 2
PEER_TOPK = 16
DEPTH = 1
ALPHA = (2.0 * DEPTH) ** 0.25
LN_EPS = 1e-5

LANES = 128
VMEM_LIMIT = 56 * 1024 * 1024
SB_SKIP_LOG = -104.0

PA_WIDTH = 3 * SB_WIDTH + 4 * ML_WIDTH


def _cparams(*sem):
    return pltpu.CompilerParams(dimension_semantics=sem, vmem_limit_bytes=VMEM_LIMIT)


def _layer_norm(x, g, b):
    mu = jnp.mean(x, axis=-1, keepdims=True)
    xc = x - mu
    var = jnp.mean(xc * xc, axis=-1, keepdims=True)
    return xc * lax.rsqrt(var + LN_EPS) * g + b


def _log_sigmoid(z):
    return jnp.minimum(z, 0.0) - jnp.log1p(jnp.exp(-jnp.abs(z)))


def _sigmoid(z):
    return 1.0 / (1.0 + jnp.exp(-z))


def _dot(a, b):
    return jnp.dot(a, b, preferred_element_type=F32)


def _dot_nt(a, b):
    return lax.dot_general(a, b, (((1,), (1,)), ((), ())), preferred_element_type=F32)


def _dot_tn(a, b):
    return lax.dot_general(a, b, (((0,), (0,)), ((), ())), preferred_element_type=F32)


def _inproj_kernel(x_ref, g_ref, b_ref, wa_ref, wif_ref, wg_ref, oa_ref, oif_ref, og_ref, *, cw):
    h = _layer_norm(x_ref[...], g_ref[...], b_ref[...]).astype(BF16)
    for j in range(0, wa_ref.shape[1], cw):
        oa_ref[:, j:j + cw] = _dot(h, wa_ref[:, j:j + cw]).astype(BF16)
    oif_ref[...] = _dot(h, wif_ref[...])
    for j in range(0, wg_ref.shape[1], cw):
        og_ref[:, j:j + cw] = _dot(h, wg_ref[:, j:j + cw]).astype(BF16)


def _inproj(x3, bi, g, b, wa, wif, wg, tm=512):
    _, n, d = x3.shape
    const = lambda i: (0, 0)
    return pl.pallas_call(
        functools.partial(_inproj_kernel, cw=512),
        grid=(n // tm,),
        in_specs=[
            pl.BlockSpec((None, tm, d), lambda i: (bi, i, 0)),
            pl.BlockSpec((1, d), const),
            pl.BlockSpec((1, d), const),
            pl.BlockSpec(wa.shape, const),
            pl.BlockSpec(wif.shape, const),
            pl.BlockSpec(wg.shape, const),
        ],
        out_specs=[
            pl.BlockSpec((tm, wa.shape[1]), lambda i: (i, 0)),
            pl.BlockSpec((tm, wif.shape[1]), lambda i: (i, 0)),
            pl.BlockSpec((tm, wg.shape[1]), lambda i: (i, 0)),
        ],
        out_shape=[
            jax.ShapeDtypeStruct((n, wa.shape[1]), BF16),
            jax.ShapeDtypeStruct((n, wif.shape[1]), F32),
            jax.ShapeDtypeStruct((n, wg.shape[1]), BF16),
        ],
        compiler_params=_cparams("parallel"),
        name="ln_inproj",
    )(x3, g, b, wa, wif, wg)


def _sb_kernel(q_ref, k_ref, v_ref, o_ref, *, tq):
    qi = pl.program_id(2)
    q = q_ref[0]
    lane = lax.broadcasted_iota(I32, (1, LANES), 1)
    row = lax.broadcasted_iota(I32, (tq, tq), 0)
    col = lax.broadcasted_iota(I32, (tq, tq), 1)
    later = (row > col).astype(BF16)
    scale = SB_HEAD_DIM ** -0.5
    first = lane < SB_HEAD_DIM
    zero = jnp.zeros_like(q)
    q2 = jnp.concatenate([jnp.where(first, q, zero), jnp.where(first, zero, q)], axis=0)
    q2 = (q2.astype(F32) * scale).astype(BF16)
    row2 = lax.broadcasted_iota(I32, (2 * tq, tq), 0)
    causal = lax.broadcasted_iota(I32, (2 * tq, tq), 1) < jnp.where(row2 >= tq, row2 - tq, row2)

    def tile(qh, kb, carry, masked):
        off = pl.multiple_of(kb * tq, tq)
        k_blk = k_ref[0, pl.ds(off, tq), :]
        v_blk = v_ref[0, pl.ds(off, tq), :]
        z = _dot_nt(qh, k_blk)
        lb = _log_sigmoid(z)
        lom = lb - z
        if masked:
            lom = jnp.where(causal, lom, 0.0)
        hi = lom.astype(BF16)
        lo = (lom - hi.astype(F32)).astype(BF16)
        tail = _dot(hi, later) + _dot(lo, later) + carry
        w = jnp.exp(lb + tail)
        if masked:
            w = jnp.where(causal, w, 0.0)
        contrib = _dot(w.astype(BF16), v_blk)
        return contrib, carry + jnp.sum(lom, axis=1, keepdims=True)

    acc, carry = tile(q2, qi, jnp.zeros((2 * tq, 1), F32), True)

    def cond(st):
        kb, _, _, cmax = st
        return jnp.logical_and(kb >= 0, cmax > SB_SKIP_LOG)

    def body(st):
        kb, acc, carry, _ = st
        contrib, carry = tile(q2, kb, carry, False)
        return kb - 1, acc + contrib, carry, jnp.max(carry)

    _, acc, _, _ = lax.while_loop(cond, body, (qi - 1, acc, carry, jnp.max(carry)))
    o_ref[0] = jnp.where(first, acc[:tq], acc[tq:]).astype(o_ref.dtype)


def _sb_attention(pa3, tq=128):
    b, s, _ = pa3.shape
    npair = SB_WIDTH // LANES
    return pl.pallas_call(
        functools.partial(_sb_kernel, tq=tq),
        grid=(b, npair, s // tq),
        in_specs=[
            pl.BlockSpec((1, tq, LANES), lambda bi, hp, qi: (bi, qi, hp)),
            pl.BlockSpec((1, s, LANES), lambda bi, hp, qi: (bi, 0, npair + hp)),
            pl.BlockSpec((1, s, LANES), lambda bi, hp, qi: (bi, 0, 2 * npair + hp)),
        ],
        out_specs=pl.BlockSpec((1, tq, LANES), lambda bi, hp, qi: (bi, qi, hp)),
        out_shape=jax.ShapeDtypeStruct((b, s, SB_WIDTH), BF16),
        compiler_params=_cparams("parallel", "parallel", "arbitrary"),
        name="sb_attention",
    )(pa3, pa3, pa3)


CONV_HALO = 16


def _conv_kernel(x_ref, prev_ref, w_ref, b_ref, o_ref, buf_ref, *, ts):
    si = pl.program_id(1)
    is_k = pl.program_id(2)
    prev = prev_ref[0].astype(F32)
    buf_ref[0:CONV_HALO, :] = jnp.where(si == 0, 0.0, prev)
    buf_ref[CONV_HALO:, :] = x_ref[0].astype(F32)
    y = b_ref[...] + w_ref[0:1, :] * buf_ref[CONV_HALO:, :]
    for j in range(1, CONV_WIDTH):
        y = y + w_ref[j:j + 1, :] * buf_ref[CONV_HALO - j:CONV_HALO - j + ts, :]
    y = y * _sigmoid(y)
    y = y * jnp.where(is_k == 1, ML_HEAD_DIM ** -0.5, 1.0)
    o_ref[0] = y.astype(o_ref.dtype)


def _conv_silu(pa3, conv_w, conv_b, ts=1024):
    b, s, _ = pa3.shape
    ts = min(ts, s)
    cb = ML_WIDTH
    base = 3 * SB_WIDTH // cb
    hb = ts // CONV_HALO
    return pl.pallas_call(
        functools.partial(_conv_kernel, ts=ts),
        grid=(b, s // ts, 2),
        in_specs=[
            pl.BlockSpec((1, ts, cb), lambda bi, si, j: (bi, si, base + j)),
            pl.BlockSpec((1, CONV_HALO, cb), lambda bi, si, j: (bi, jnp.maximum(si * hb - 1, 0), base + j)),
            pl.BlockSpec((CONV_WIDTH, cb), lambda bi, si, j: (0, j)),
            pl.BlockSpec((1, cb), lambda bi, si, j: (0, j)),
        ],
        out_specs=pl.BlockSpec((1, ts, cb), lambda bi, si, j: (bi, si, j)),
        out_shape=jax.ShapeDtypeStruct((b, s, 2 * ML_WIDTH), BF16),
        scratch_shapes=[pltpu.VMEM((ts + CONV_HALO, cb), F32)],
        compiler_params=_cparams("parallel", "parallel", "parallel"),
        name="conv_silu",
    )(pa3, pa3, conv_w, conv_b)


def _mlstm_kernel(q_ref, k_ref, v_ref, g_ref, gb_ref, o_ref):
    L = ML_CHUNK
    head = pl.program_id(1)
    nc = q_ref.shape[1] // L
    r = lax.broadcasted_iota(I32, (L, L), 0)
    c = lax.broadcasted_iota(I32, (L, L), 1)
    eye = r == c
    sub = lax.broadcasted_iota(I32, (2 * ML_HEADS, L), 0)

    def to_col(row):
        return jnp.sum(jnp.where(eye, jnp.broadcast_to(row, (L, L)), 0.0), axis=1, keepdims=True)

    def chunk(ci, st):
        c_st, n_st, m_st = st
        off = pl.multiple_of(ci * L, L)
        q = q_ref[0, pl.ds(off, L), :]
        k = k_ref[0, pl.ds(off, L), :]
        v = v_ref[0, pl.ds(off, L), :]
        g = g_ref[0, :, pl.ds(off, L)] + gb_ref[...]
        li_row = jnp.sum(jnp.where(sub == head, g, 0.0), axis=0, keepdims=True)
        lf_row = _log_sigmoid(jnp.sum(jnp.where(sub == head + ML_HEADS, g, 0.0), axis=0, keepdims=True))
        lf_b = jnp.broadcast_to(lf_row, (L, L))
        bcum_col = jnp.sum(jnp.where(c <= r, lf_b, 0.0), axis=1, keepdims=True)
        lf_col = to_col(lf_row)
        li_col = to_col(li_row)
        bcum_row = jnp.sum(jnp.where(r <= c, jnp.broadcast_to(lf_col, (L, L)), 0.0), axis=0, keepdims=True)
        b_last = jnp.sum(lf_row, axis=1, keepdims=True)

        d_log = jnp.where(c <= r, bcum_col - bcum_row + li_row, -jnp.inf)
        inter = bcum_col + m_st
        m_t = jnp.maximum(inter, jnp.max(d_log, axis=1, keepdims=True))
        a_t = jnp.exp(inter - m_t)
        s_w = _dot_nt(q, k) * jnp.exp(d_log - m_t)
        num = a_t * _dot_nt(q, c_st.astype(BF16)) + _dot(s_w.astype(BF16), v)
        den = a_t * jnp.sum(q.astype(F32) * n_st, axis=1, keepdims=True) + jnp.sum(s_w, axis=1, keepdims=True)
        o_ref[0, pl.ds(off, L), :] = (num / jnp.maximum(jnp.abs(den), jnp.exp(-m_t))).astype(o_ref.dtype)

        w_end = b_last - bcum_col + li_col
        m_loc = jnp.max(w_end, axis=0, keepdims=True)
        e_end = jnp.exp(w_end - m_loc)
        c_loc = _dot_tn((e_end * v.astype(F32)).astype(BF16), k)
        n_loc = jnp.sum(e_end * k.astype(F32), axis=0, keepdims=True)
        m_new = jnp.maximum(b_last + m_st, m_loc)
        a = jnp.exp(b_last + m_st - m_new)
        gg = jnp.exp(m_loc - m_new)
        return a * c_st + gg * c_loc, a * n_st + gg * n_loc, m_new

    init = (jnp.zeros((ML_HEAD_DIM, ML_HEAD_DIM), F32), jnp.zeros((1, ML_HEAD_DIM), F32), jnp.zeros((1, 1), F32))
    lax.fori_loop(0, nc, chunk, init)


def _mlstm(qk3, pa3, grow, gbias):
    b, s, _ = pa3.shape
    vbase = (3 * SB_WIDTH + 2 * ML_WIDTH) // ML_HEAD_DIM
    return pl.pallas_call(
        _mlstm_kernel,
        grid=(b, ML_HEADS),
        in_specs=[
            pl.BlockSpec((1, s, ML_HEAD_DIM), lambda bi, h: (bi, 0, h)),
            pl.BlockSpec((1, s, ML_HEAD_DIM), lambda bi, h: (bi, 0, ML_HEADS + h)),
            pl.BlockSpec((1, s, ML_HEAD_DIM), lambda bi, h: (bi, 0, vbase + h)),
            pl.BlockSpec((1, 2 * ML_HEADS, s), lambda bi, h: (bi, 0, 0)),
            pl.BlockSpec((2 * ML_HEADS, 1), lambda bi, h: (0, 0)),
        ],
        out_specs=pl.BlockSpec((1, s, ML_HEAD_DIM), lambda bi, h: (bi, 0, h)),
        out_shape=jax.ShapeDtypeStruct((b, s, ML_WIDTH), BF16),
        compiler_params=_cparams("parallel", "parallel"),
        name="mlstm",
    )(qk3, qk3, pa3, grow, gbias)


def _merge_kernel(x_ref, osb_ref, hml_ref, mlo_ref, gate_ref, g0_ref, b0_ref, wsb_ref, wml_ref, wout_ref,
                  g1_ref, b1_ref, o_ref):
    h0 = _layer_norm(x_ref[...], g0_ref[...], b0_ref[...])
    o_ml = (_sigmoid(mlo_ref[...].astype(F32)) * hml_ref[...].astype(F32)).astype(BF16)
    y = _sigmoid(gate_ref[:, :D_MODEL].astype(F32)) * _dot(osb_ref[...], wsb_ref[...])
    y = y + _sigmoid(gate_ref[:, D_MODEL:].astype(F32)) * _dot(o_ml, wml_ref[...])
    mix = _dot(y.astype(BF16), wout_ref[...])
    o_ref[...] = _layer_norm(ALPHA * h0 + mix, g1_ref[...], b1_ref[...])


def _merge(x3, bi, osb, hml, pa, gate, g0, b0, wsb, wml, wout, g1, b1, tm=512):
    _, n, d = x3.shape
    const = lambda i: (0, 0)
    rowblk = lambda w: pl.BlockSpec((tm, w), lambda i: (i, 0))
    vec = pl.BlockSpec((1, d), const)
    mlo_blk = (3 * SB_WIDTH + 3 * ML_WIDTH) // ML_WIDTH
    return pl.pallas_call(
        _merge_kernel,
        grid=(n // tm,),
        in_specs=[
            pl.BlockSpec((None, tm, d), lambda i: (bi, i, 0)), rowblk(SB_WIDTH), rowblk(ML_WIDTH),
            pl.BlockSpec((tm, ML_WIDTH), lambda i: (i, mlo_blk)),
            rowblk(2 * d), vec, vec,
            pl.BlockSpec(wsb.shape, const), pl.BlockSpec(wml.shape, const), pl.BlockSpec(wout.shape, const),
            vec, vec,
        ],
        out_specs=rowblk(d),
        out_shape=jax.ShapeDtypeStruct((n, d), F32),
        compiler_params=_cparams("parallel"),
        name="merge_outproj_ln",
    )(x3, osb, hml, pa, gate, g0, b0, wsb, wml, wout, g1, b1)


def _topk_rows(s, k, ids=None):
    if ids is None:
        ids = lax.broadcasted_iota(I32, s.shape, 0)
    none = jnp.iinfo(jnp.int32).max
    vals, idxs = [], []
    for _ in range(k):
        m = jnp.max(s, axis=0, keepdims=True)
        am = jnp.min(jnp.where(s == m, ids, none), axis=0, keepdims=True)
        vals.append(m)
        idxs.append(am)
        s = jnp.where(ids == am, -jnp.inf, s)
    return jnp.concatenate(vals, axis=0), jnp.concatenate(idxs, axis=0)


SUBLANES = 8


def _pair_candidates(v1, v2):
    k = PEER_TOPK
    sub = lax.broadcasted_iota(I32, (SUBLANES, v1.shape[1]), 0)
    vals, ids = [], []
    i = 0
    while k // (i + 1) > 1:
        nj = k // (i + 1)
        for j0 in range(0, nj, SUBLANES):
            val = v1[i:i + 1, :] + v2[j0:j0 + SUBLANES, :]
            if nj - j0 < SUBLANES:
                val = jnp.where(sub < nj - j0, val, -jnp.inf)
            vals.append(val)
            ids.append(i * k + j0 + sub)
        i += 1
    assert (k - i) % SUBLANES == 0
    for i0 in range(i, k, SUBLANES):
        vals.append(v1[i0:i0 + SUBLANES, :] + v2[0:1, :])
        ids.append((i0 + sub) * k)
    return jnp.concatenate(vals, axis=0), jnp.concatenate(ids, axis=0)


def _select_rows(sel, table):
    out = jnp.zeros(sel.shape, table.dtype)
    for r_ in range(table.shape[0]):
        out = jnp.where(sel == r_, table[r_:r_ + 1, :], out)
    return out


def _peer_route_kernel(h_ref, wq_ref, k1_ref, k2_ref, idx_ref, gate_ref, q_scr, *, tt):
    q_scr[...] = _dot(h_ref[...].astype(BF16), wq_ref[...]).astype(BF16)
    nsub = h_ref.shape[0] // tt

    def one(it, _):
        head = it % PEER_HEADS
        sub = it // PEER_HEADS
        roff = pl.multiple_of(sub * tt, tt)
        coff = pl.multiple_of(head * PEER_QDIM, PEER_QDIM)
        q1 = q_scr[pl.ds(roff, tt), pl.ds(coff, PEER_HALF)]
        q2 = q_scr[pl.ds(roff, tt), pl.ds(coff + PEER_HALF, PEER_HALF)]
        v1, i1 = _topk_rows(_dot_nt(k1_ref[...], q1), PEER_TOPK)
        v2, i2 = _topk_rows(_dot_nt(k2_ref[...], q2), PEER_TOPK)
        cand, cand_pos = _pair_candidates(v1, v2)
        top_s, pos = _topk_rows(cand, PEER_TOPK, cand_pos)
        e1 = _select_rows(pos // PEER_TOPK, i1)
        e2 = _select_rows(pos % PEER_TOPK, i2)
        ex = jnp.exp(top_s - top_s[0:1, :])
        gates = ex / jnp.sum(ex, axis=0, keepdims=True)
        hoff = pl.multiple_of(head * PEER_TOPK, PEER_TOPK)
        idx_ref[pl.ds(hoff, PEER_TOPK), pl.ds(roff, tt)] = e1 * PEER_KEYS + e2
        gate_ref[pl.ds(hoff, PEER_TOPK), pl.ds(roff, tt)] = gates
        return 0

    lax.fori_loop(0, nsub * PEER_HEADS, one, 0)


def _peer_route(h1, tok0, n, wq, k1, k2, tm=256, tt=128):
    d = h1.shape[1]
    off = tok0 // tm
    const = lambda i: (0, 0)
    nsel = PEER_HEADS * PEER_TOPK
    return pl.pallas_call(
        functools.partial(_peer_route_kernel, tt=tt),
        grid=(n // tm,),
        in_specs=[
            pl.BlockSpec((tm, d), lambda i: (off + i, 0)),
            pl.BlockSpec(wq.shape, const),
            pl.BlockSpec(k1.shape, const),
            pl.BlockSpec(k2.shape, const),
        ],
        out_specs=[pl.BlockSpec((nsel, tm), lambda i: (0, i)), pl.BlockSpec((nsel, tm), lambda i: (0, i))],
        out_shape=[jax.ShapeDtypeStruct((nsel, n), I32), jax.ShapeDtypeStruct((nsel, n), F32)],
        scratch_shapes=[pltpu.VMEM((tm, PEER_HEADS * PEER_QDIM), BF16)],
        compiler_params=_cparams("parallel"),
        name="peer_route",
    )(h1, wq, k1, k2)


def _dot_split3(x, sel):
    hi = x.astype(BF16)
    r1 = x - hi.astype(F32)
    mid = r1.astype(BF16)
    lo = (r1 - mid.astype(F32)).astype(BF16)
    return _dot(hi, sel) + _dot(mid, sel) + _dot(lo, sel)


def _gelu_gate_kernel(part_ref, gate_ref, sel_ref, selt_ref, o_ref):
    a = _dot_split3(part_ref[...], sel_ref[...])
    w = gate_ref[...] * (0.5 * a * (1.0 + lax.erf(a * (2.0 ** -0.5))))
    o_ref[...] = _dot_split3(w, selt_ref[...])


def _gelu_gate(part, gates, tm=512):
    n, wide = part.shape
    nsel = gates.shape[1]
    lanes = wide // nsel
    sel = (jnp.arange(wide)[:, None] // lanes == jnp.arange(nsel)[None, :]).astype(BF16)
    const = lambda i: (0, 0)
    return pl.pallas_call(
        _gelu_gate_kernel,
        grid=(n // tm,),
        in_specs=[pl.BlockSpec((tm, wide), lambda i: (i, 0)), pl.BlockSpec((tm, nsel), lambda i: (i, 0)),
                  pl.BlockSpec((wide, nsel), const), pl.BlockSpec((nsel, wide), const)],
        out_specs=pl.BlockSpec((tm, wide), lambda i: (i, 0)),
        out_shape=jax.ShapeDtypeStruct((n, wide), F32),
        compiler_params=_cparams("parallel"),
        name="gelu_gate",
    )(part, gates, sel, sel.T)


def _final_kernel(h_ref, ffn_ref, p_ref, wg_ref, wp_ref, g_ref, b_ref, o_ref):
    h = h_ref[...]
    ple = _sigmoid(_dot(h.astype(BF16), wg_ref[...])) * _dot(p_ref[...].astype(BF16), wp_ref[...])
    o_ref[...] = _layer_norm(ALPHA * h + ffn_ref[...] + ple, g_ref[...], b_ref[...])


def _final(h1, tok0, ffn, p4, bi, wg, wp, g, b, tm=512):
    n, d = ffn.shape
    off = tok0 // tm
    const = lambda i: (0, 0)
    rowblk = lambda w: pl.BlockSpec((tm, w), lambda i: (i, 0))
    vec = pl.BlockSpec((1, d), const)
    return pl.pallas_call(
        _final_kernel,
        grid=(n // tm,),
        in_specs=[pl.BlockSpec((tm, d), lambda i: (off + i, 0)), rowblk(d),
                  pl.BlockSpec((None, None, tm, p4.shape[3]), lambda i: (0, bi, off + i, 0)),
                  pl.BlockSpec(wg.shape, const), pl.BlockSpec(wp.shape, const), vec, vec],
        out_specs=rowblk(d),
        out_shape=jax.ShapeDtypeStruct((n, d), F32),
        compiler_params=_cparams("parallel"),
        name="ple_final_ln",
    )(h1, ffn, p4, wg, wp, g, b)


SC_CORES = 2
SC_SUBCORES = 16
SC_LANES = 16
SC_WORKERS = SC_CORES * SC_SUBCORES
PEER_NSEL = PEER_HEADS * PEER_TOPK
SC_ROWS = 16
SC_NBUF = 4
SC_GROUP = 8


def _sc_mesh():
    return plsc.VectorSubcoreMesh(core_axis_name="c", subcore_axis_name="s", num_cores=SC_CORES,
                                  num_subcores=SC_SUBCORES)


def _sc_grouped_pipeline(ngroups, stage_copies, out_copy, tab_hbm, idx_v, buf, sems, begin_group, compute):
    nchunk = PEER_NSEL // SC_ROWS
    nsteps = SC_GROUP * nchunk
    assert nsteps % SC_NBUF == 0 and ngroups % 2 == 0

    def gather(gs, step, slot):
        t = step // nchunk
        c = step % nchunk
        return pltpu.make_async_copy(tab_hbm.at[idx_v.at[gs, t, pl.ds(c * SC_ROWS, SC_ROWS)]], buf.at[slot],
                                     sems.at[slot])

    for cp in stage_copies(0, 0):
        cp.start()
    for cp in stage_copies(0, 0):
        cp.wait()
    for b in range(SC_NBUF - 1):
        gather(0, b, b).start()
    for cp in stage_copies(1, 1):
        cp.start()

    def group_pair(gp, _):
        for gs in range(2):
            g = 2 * gp + gs

            @pl.when(g >= 2)
            def _():
                out_copy(g - 2, gs).wait()

            begin_group(gs)

            def ring_round(q, _):
                for b in range(SC_NBUF):
                    s = q * SC_NBUF + b
                    ahead = s + SC_NBUF - 1
                    slot_a = (b + SC_NBUF - 1) % SC_NBUF

                    @pl.when(ahead < nsteps)
                    def _():
                        gather(gs, ahead, slot_a).start()

                    if b > 0:
                        @pl.when(jnp.logical_and(ahead >= nsteps, g + 1 < ngroups))
                        def _():
                            if b == 1:
                                for cp in stage_copies(g + 1, 1 - gs):
                                    cp.wait()
                            gather(1 - gs, ahead - nsteps, slot_a).start()

                    gather(gs, s, b).wait()
                    compute(gs, s // nchunk, s % nchunk, b)
                return 0

            lax.fori_loop(0, nsteps // SC_NBUF, ring_round, 0)
            out_copy(g, gs).start()

            @pl.when(g + 2 < ngroups)
            def _():
                for cp in stage_copies(g + 2, gs):
                    cp.start()
        return 0

    lax.fori_loop(0, ngroups // 2, group_pair, 0)
    out_copy(ngroups - 2, 0).wait()
    out_copy(ngroups - 1, 1).wait()


def _peer_act_sc(h1, tok0, idx, u_tab):
    n = idx.shape[0]
    d = h1.shape[1]
    per_w = n // SC_WORKERS
    hold = 32

    @functools.partial(
        pl.kernel, mesh=_sc_mesh(),
        out_type=jax.ShapeDtypeStruct((n, PEER_NSEL * SC_LANES), F32),
        scratch_types=[
            pltpu.VMEM((2, SC_GROUP, d), F32),
            pltpu.VMEM((2, SC_GROUP, PEER_NSEL), I32),
            pltpu.VMEM((2, SC_GROUP, PEER_NSEL * SC_LANES), F32),
            pltpu.VMEM((SC_NBUF, SC_ROWS, d), F32),
            pltpu.SemaphoreType.DMA((SC_NBUF,)),
            pltpu.SemaphoreType.DMA((2,)),
            pltpu.SemaphoreType.DMA((2,)),
            pltpu.SemaphoreType.DMA((2,)),
        ],
        name="peer_act_sc",
    )
    def k(h_hbm, idx_hbm, u_hbm, out_hbm, h_v, idx_v, out_v, buf, sems, sem_h, sem_i, sem_o):
        base = (lax.axis_index("s") * SC_CORES + lax.axis_index("c")) * per_w

        def stage_copies(g, gs):
            row0 = base + g * SC_GROUP
            return (pltpu.make_async_copy(h_hbm.at[pl.ds(tok0 + row0, SC_GROUP)], h_v.at[gs], sem_h.at[gs]),
                    pltpu.make_async_copy(idx_hbm.at[pl.ds(row0, SC_GROUP)], idx_v.at[gs], sem_i.at[gs]))

        def out_copy(g, gs):
            return pltpu.make_async_copy(out_v.at[gs], out_hbm.at[pl.ds(base + g * SC_GROUP, SC_GROUP)],
                                         sem_o.at[gs])

        def compute(gs, t, c, slot):
            for jb in range(d // (hold * SC_LANES)):
                col0 = jb * hold * SC_LANES
                hv = [h_v[gs, t, pl.ds(col0 + jj * SC_LANES, SC_LANES)] for jj in range(hold)]

                @plsc.parallel_loop(0, SC_ROWS)
                def _(r):
                    ps = [buf[slot, r, pl.ds(col0 + jj * SC_LANES, SC_LANES)] * hv[jj] for jj in range(hold)]
                    while len(ps) > 1:
                        ps = [ps[i] + ps[i + 1] for i in range(0, len(ps), 2)]
                    dst = out_v.at[gs, t, pl.ds((c * SC_ROWS + r) * SC_LANES, SC_LANES)]
                    if jb == 0:
                        dst[...] = ps[0]
                    else:
                        plsc.addupdate(dst, ps[0])

        _sc_grouped_pipeline(per_w // SC_GROUP, stage_copies, out_copy, u_hbm, idx_v, buf, sems, lambda gs: None,
                             compute)

    return k(h1, idx, u_tab)


def _peer_combine_sc(wexp, idx, v_tab):
    n = idx.shape[0]
    d = v_tab.shape[1]
    per_w = n // SC_WORKERS
    nvec = d // SC_LANES
    ncol = 4

    @functools.partial(
        pl.kernel, mesh=_sc_mesh(),
        out_type=jax.ShapeDtypeStruct((n, d), F32),
        scratch_types=[
            pltpu.VMEM((2, SC_GROUP, PEER_NSEL * SC_LANES), F32),
            pltpu.VMEM((2, SC_GROUP, PEER_NSEL), I32),
            pltpu.VMEM((2, SC_GROUP, d), F32),
            pltpu.VMEM((SC_NBUF, SC_ROWS, d), F32),
            pltpu.SemaphoreType.DMA((SC_NBUF,)),
            pltpu.SemaphoreType.DMA((2,)),
            pltpu.SemaphoreType.DMA((2,)),
            pltpu.SemaphoreType.DMA((2,)),
        ],
        name="peer_combine_sc",
    )
    def k(w_hbm, idx_hbm, v_hbm, out_hbm, w_v, idx_v, o_v, buf, sems, sem_w, sem_i, sem_o):
        base = (lax.axis_index("s") * SC_CORES + lax.axis_index("c")) * per_w

        def stage_copies(g, gs):
            row0 = base + g * SC_GROUP
            return (pltpu.make_async_copy(w_hbm.at[pl.ds(row0, SC_GROUP)], w_v.at[gs], sem_w.at[gs]),
                    pltpu.make_async_copy(idx_hbm.at[pl.ds(row0, SC_GROUP)], idx_v.at[gs], sem_i.at[gs]))

        def out_copy(g, gs):
            return pltpu.make_async_copy(o_v.at[gs], out_hbm.at[pl.ds(base + g * SC_GROUP, SC_GROUP)], sem_o.at[gs])

        def begin_group(gs):
            def zero(i, _):
                o_v[gs, i // nvec, pl.ds((i % nvec) * SC_LANES, SC_LANES)] = jnp.zeros((SC_LANES,), F32)
                return 0

            lax.fori_loop(0, SC_GROUP * nvec, zero, 0)

        def compute(gs, t, c, slot):
            ws = [w_v[gs, t, pl.ds((c * SC_ROWS + r) * SC_LANES, SC_LANES)] for r in range(SC_ROWS)]

            @plsc.parallel_loop(0, nvec // ncol)
            def _(cb):
                col0 = cb * ncol * SC_LANES
                acc = [None] * ncol
                for r in range(SC_ROWS):
                    for kk in range(ncol):
                        x = ws[r] * buf[slot, r, pl.ds(col0 + kk * SC_LANES, SC_LANES)]
                        acc[kk] = x if acc[kk] is None else acc[kk] + x
                for kk in range(ncol):
                    plsc.addupdate(o_v.at[gs, t, pl.ds(col0 + kk * SC_LANES, SC_LANES)], acc[kk])

        _sc_grouped_pipeline(per_w // SC_GROUP, stage_copies, out_copy, v_hbm, idx_v, buf, sems, begin_group, compute)

    return k(wexp, idx, v_tab)


def _token_mixer_and_norm(x3, bi, g0, b0, wa, wif, wg, gbias, conv_w, conv_b, wsb, wml, wout, g1, b1):
    s = x3.shape[1]
    n_if = 2 * ML_HEADS
    pa, gif, gate = _inproj(x3, bi, g0, b0, wa, wif, wg)
    pa3 = pa.reshape(1, s, PA_WIDTH)
    o_sb = _sb_attention(pa3)
    qk3 = _conv_silu(pa3, conv_w, conv_b)
    grow = jnp.swapaxes(gif.reshape(1, s, LANES)[:, :, :n_if], 1, 2)
    h_ml = _mlstm(qk3, pa3, grow, gbias)
    return _merge(x3, bi, o_sb.reshape(s, SB_WIDTH), h_ml.reshape(s, ML_WIDTH), pa, gate, g0, b0, wsb, wml, wout,
                  g1, b1)


def _after(value, prev):
    if prev is None:
        return value
    return lax.optimization_barrier((value, prev))[0]


def kernel(x, p, ln0_g, ln0_b, w_in, b_igate, b_fgate, conv_w, conv_b, w_branch_sb, w_branch_ml, w_out, ln1_g, ln1_b, peer_wq, peer_k1, peer_k2, peer_u, peer_v, w_ple_gate, w_ple, ln2_g, ln2_b):
    b, s, d = x.shape
    assert w_in.shape[0] == DEPTH
    i = 0
    row = lambda v: v.reshape(1, -1)
    n_if = 2 * ML_HEADS
    wa = w_in[i][:, :PA_WIDTH].astype(BF16)
    wif = jnp.pad(w_in[i][:, PA_WIDTH:PA_WIDTH + n_if], ((0, 0), (0, LANES - n_if))).astype(BF16)
    wg = w_in[i][:, PA_WIDTH + n_if:].astype(BF16)
    gbias = jnp.concatenate([b_igate[i], b_fgate[i]]).reshape(n_if, 1)
    wsb, wml, wout = w_branch_sb[i].astype(BF16), w_branch_ml[i].astype(BF16), w_out[i].astype(BF16)
    wq, k1, k2 = peer_wq[i].astype(BF16), peer_k1[i].astype(BF16), peer_k2[i].astype(BF16)
    wpg, wp = w_ple_gate[i].astype(BF16), w_ple[i].astype(BF16)

    halves = 2
    sp = s // halves
    h1s, routed, parts, ffns, outs = {}, {}, {}, {}, {}
    last = None

    def mix(e):
        h1s[e] = _token_mixer_and_norm(x, e, _after(row(ln0_g), last), row(ln0_b), wa, wif, wg, gbias, conv_w[i],
                                       row(conv_b[i]), wsb, wml, wout, row(ln1_g[i]), row(ln1_b[i]))
        return h1s[e]

    def route(pc):
        e, hf = divmod(pc, halves)
        idx_t, gate_t = _peer_route(_after(h1s[e], last), hf * sp, sp, wq, k1, k2)
        idx = idx_t.T
        routed[pc] = (idx, gate_t.T)
        parts[pc] = _peer_act_sc(h1s[e], hf * sp, _after(idx, ffns.get(pc - 2)), peer_u[i])
        return idx_t

    def gelu(pc):
        e, hf = divmod(pc, halves)
        idx, gates = routed[pc]
        wexp = _gelu_gate(parts[pc], _after(gates, last))
        ffns[pc] = _peer_combine_sc(wexp, idx, peer_v[i])
        outs[pc] = _final(h1s[e], hf * sp, ffns[pc], p, e, wpg, wp, row(ln2_g[i]), row(ln2_b[i]))
        return wexp

    order = [(mix, 0), (route, 0), (route, 1)]
    for e in range(1, b):
        order += [(mix, e), (gelu, 2 * e - 2), (route, 2 * e), (gelu, 2 * e - 1), (route, 2 * e + 1)]
    order += [(gelu, 2 * b - 2), (gelu, 2 * b - 1)]
    for stage, arg in order:
        last = stage(arg)
    return jnp.concatenate([outs[pc] for pc in range(halves * b)]).reshape(b, s, d)
```

```python
import functools

import jax
import jax.numpy as jnp
from jax import lax
from jax.experimental import pallas as pl
from jax.experimental.pallas import tpu as pltpu
from jax.experimental.pallas import tpu_sc as plsc

F32 = jnp.float32
BF16 = jnp.bfloat16
I32 = jnp.int32

D_MODEL = 1024
SB_HEADS = 8
SB_HEAD_DIM = 64
SB_WIDTH = SB_HEADS * SB_HEAD_DIM
ML_HEADS = 4
ML_HEAD_DIM = 128
ML_WIDTH = ML_HEADS * ML_HEAD_DIM
ML_CHUNK = 128
CONV_WIDTH = 4
PEER_HEADS = 8
PEER_KEYS = 128
PEER_QDIM = 256
PEER_HALF = PEER_QDIM // 2
PEER_TOPK = 16
DEPTH = 1
ALPHA = (2.0 * DEPTH) ** 0.25
LN_EPS = 1e-5

LANES = 128
VMEM_LIMIT = 56 * 1024 * 1024
SB_SKIP_LOG = -104.0

PA_WIDTH = 3 * SB_WIDTH + 4 * ML_WIDTH


def _cparams(*sem):
    return pltpu.CompilerParams(dimension_semantics=sem, vmem_limit_bytes=VMEM_LIMIT)


def _layer_norm(x, g, b):
    mu = jnp.mean(x, axis=-1, keepdims=True)
    xc = x - mu
    var = jnp.mean(xc * xc, axis=-1, keepdims=True)
    return xc * lax.rsqrt(var + LN_EPS) * g + b


def _log_sigmoid(z):
    return jnp.minimum(z, 0.0) - jnp.log1p(jnp.exp(-jnp.abs(z)))


def _sigmoid(z):
    return 1.0 / (1.0 + jnp.exp(-z))


def _dot(a, b):
    return jnp.dot(a, b, preferred_element_type=F32)


def _dot_nt(a, b):
    return lax.dot_general(a, b, (((1,), (1,)), ((), ())), preferred_element_type=F32)


def _dot_tn(a, b):
    return lax.dot_general(a, b, (((0,), (0,)), ((), ())), preferred_element_type=F32)


def _inproj_kernel(x_ref, g_ref, b_ref, wa_ref, wif_ref, wg_ref, oa_ref, oif_ref, og_ref, *, cw):
    h = _layer_norm(x_ref[...], g_ref[...], b_ref[...]).astype(BF16)
    for j in range(0, wa_ref.shape[1], cw):
        oa_ref[:, j:j + cw] = _dot(h, wa_ref[:, j:j + cw]).astype(BF16)
    oif_ref[...] = _dot(h, wif_ref[...])
    for j in range(0, wg_ref.shape[1], cw):
        og_ref[:, j:j + cw] = _dot(h, wg_ref[:, j:j + cw]).astype(BF16)


def _inproj(x3, bi, g, b, wa, wif, wg, tm=512):
    _, n, d = x3.shape
    const = lambda i: (0, 0)
    return pl.pallas_call(
        functools.partial(_inproj_kernel, cw=512),
        grid=(n // tm,),
        in_specs=[
            pl.BlockSpec((None, tm, d), lambda i: (bi, i, 0)),
            pl.BlockSpec((1, d), const),
            pl.BlockSpec((1, d), const),
            pl.BlockSpec(wa.shape, const),
            pl.BlockSpec(wif.shape, const),
            pl.BlockSpec(wg.shape, const),
        ],
        out_specs=[
            pl.BlockSpec((tm, wa.shape[1]), lambda i: (i, 0)),
            pl.BlockSpec((tm, wif.shape[1]), lambda i: (i, 0)),
            pl.BlockSpec((tm, wg.shape[1]), lambda i: (i, 0)),
        ],
        out_shape=[
            jax.ShapeDtypeStruct((n, wa.shape[1]), BF16),
            jax.ShapeDtypeStruct((n, wif.shape[1]), F32),
            jax.ShapeDtypeStruct((n, wg.shape[1]), BF16),
        ],
        compiler_params=_cparams("parallel"),
        name="ln_inproj",
    )(x3, g, b, wa, wif, wg)


def _sb_kernel(q_ref, k_ref, v_ref, o_ref, *, tq):
    qi = pl.program_id(2)
    q = q_ref[0]
    lane = lax.broadcasted_iota(I32, (1, LANES), 1)
    row = lax.broadcasted_iota(I32, (tq, tq), 0)
    col = lax.broadcasted_iota(I32, (tq, tq), 1)
    later = (row > col).astype(BF16)
    scale = SB_HEAD_DIM ** -0.5
    first = lane < SB_HEAD_DIM
    zero = jnp.zeros_like(q)
    q2 = jnp.concatenate([jnp.where(first, q, zero), jnp.where(first, zero, q)], axis=0)
    q2 = (q2.astype(F32) * scale).astype(BF16)
    row2 = lax.broadcasted_iota(I32, (2 * tq, tq), 0)
    causal = lax.broadcasted_iota(I32, (2 * tq, tq), 1) < jnp.where(row2 >= tq, row2 - tq, row2)

    def tile(qh, kb, carry, masked):
        off = pl.multiple_of(kb * tq, tq)
        k_blk = k_ref[0, pl.ds(off, tq), :]
        v_blk = v_ref[0, pl.ds(off, tq), :]
        z = _dot_nt(qh, k_blk)
        lb = _log_sigmoid(z)
        lom = lb - z
        if masked:
            lom = jnp.where(causal, lom, 0.0)
        hi = lom.astype(BF16)
        lo = (lom - hi.astype(F32)).astype(BF16)
        tail = _dot(hi, later) + _dot(lo, later) + carry
        w = jnp.exp(lb + tail)
        if masked:
            w = jnp.where(causal, w, 0.0)
        contrib = _dot(w.astype(BF16), v_blk)
        return contrib, carry + jnp.sum(lom, axis=1, keepdims=True)

    acc, carry = tile(q2, qi, jnp.zeros((2 * tq, 1), F32), True)

    def cond(st):
        kb, _, _, cmax = st
        return jnp.logical_and(kb >= 0, cmax > SB_SKIP_LOG)

    def body(st):
        kb, acc, carry, _ = st
        contrib, carry = tile(q2, kb, carry, False)
        return kb - 1, acc + contrib, carry, jnp.max(carry)

    _, acc, _, _ = lax.while_loop(cond, body, (qi - 1, acc, carry, jnp.max(carry)))
    o_ref[0] = jnp.where(first, acc[:tq], acc[tq:]).astype(o_ref.dtype)


def _sb_attention(pa3, tq=128):
    b, s, _ = pa3.shape
    npair = SB_WIDTH // LANES
    return pl.pallas_call(
        functools.partial(_sb_kernel, tq=tq),
        grid=(b, npair, s // tq),
        in_specs=[
            pl.BlockSpec((1, tq, LANES), lambda bi, hp, qi: (bi, qi, hp)),
            pl.BlockSpec((1, s, LANES), lambda bi, hp, qi: (bi, 0, npair + hp)),
            pl.BlockSpec((1, s, LANES), lambda bi, hp, qi: (bi, 0, 2 * npair + hp)),
        ],
        out_specs=pl.BlockSpec((1, tq, LANES), lambda bi, hp, qi: (bi, qi, hp)),
        out_shape=jax.ShapeDtypeStruct((b, s, SB_WIDTH), BF16),
        compiler_params=_cparams("parallel", "parallel", "arbitrary"),
        name="sb_attention",
    )(pa3, pa3, pa3)


CONV_HALO = 16


def _conv_kernel(x_ref, prev_ref, w_ref, b_ref, o_ref, buf_ref, *, ts):
    si = pl.program_id(1)
    is_k = pl.program_id(2)
    prev = prev_ref[0].astype(F32)
    buf_ref[0:CONV_HALO, :] = jnp.where(si == 0, 0.0, prev)
    buf_ref[CONV_HALO:, :] = x_ref[0].astype(F32)
    y = b_ref[...] + w_ref[0:1, :] * buf_ref[CONV_HALO:, :]
    for j in range(1, CONV_WIDTH):
        y = y + w_ref[j:j + 1, :] * buf_ref[CONV_HALO - j:CONV_HALO - j + ts, :]
    y = y * _sigmoid(y)
    y = y * jnp.where(is_k == 1, ML_HEAD_DIM ** -0.5, 1.0)
    o_ref[0] = y.astype(o_ref.dtype)


def _conv_silu(pa3, conv_w, conv_b, ts=1024):
    b, s, _ = pa3.shape
    ts = min(ts, s)
    cb = ML_WIDTH
    base = 3 * SB_WIDTH // cb
    hb = ts // CONV_HALO
    return pl.pallas_call(
        functools.partial(_conv_kernel, ts=ts),
        grid=(b, s // ts, 2),
        in_specs=[
            pl.BlockSpec((1, ts, cb), lambda bi, si, j: (bi, si, base + j)),
            pl.BlockSpec((1, CONV_HALO, cb), lambda bi, si, j: (bi, jnp.maximum(si * hb - 1, 0), base + j)),
            pl.BlockSpec((CONV_WIDTH, cb), lambda bi, si, j: (0, j)),
            pl.BlockSpec((1, cb), lambda bi, si, j: (0, j)),
        ],
        out_specs=pl.BlockSpec((1, ts, cb), lambda bi, si, j: (bi, si, j)),
        out_shape=jax.ShapeDtypeStruct((b, s, 2 * ML_WIDTH), BF16),
        scratch_shapes=[pltpu.VMEM((ts + CONV_HALO, cb), F32)],
        compiler_params=_cparams("parallel", "parallel", "parallel"),
        name="conv_silu",
    )(pa3, pa3, conv_w, conv_b)


def _mlstm_kernel(q_ref, k_ref, v_ref, g_ref, gb_ref, o_ref):
    L = ML_CHUNK
    head = pl.program_id(1)
    nc = q_ref.shape[1] // L
    r = lax.broadcasted_iota(I32, (L, L), 0)
    c = lax.broadcasted_iota(I32, (L, L), 1)
    eye = r == c
    sub = lax.broadcasted_iota(I32, (2 * ML_HEADS, L), 0)

    def to_col(row):
        return jnp.sum(jnp.where(eye, jnp.broadcast_to(row, (L, L)), 0.0), axis=1, keepdims=True)

    def chunk(ci, st):
        c_st, n_st, m_st = st
        off = pl.multiple_of(ci * L, L)
        q = q_ref[0, pl.ds(off, L), :]
        k = k_ref[0, pl.ds(off, L), :]
        v = v_ref[0, pl.ds(off, L), :]
        g = g_ref[0, :, pl.ds(off, L)] + gb_ref[...]
        li_row = jnp.sum(jnp.where(sub == head, g, 0.0), axis=0, keepdims=True)
        lf_row = _log_sigmoid(jnp.sum(jnp.where(sub == head + ML_HEADS, g, 0.0), axis=0, keepdims=True))
        lf_b = jnp.broadcast_to(lf_row, (L, L))
        bcum_col = jnp.sum(jnp.where(c <= r, lf_b, 0.0), axis=1, keepdims=True)
        lf_col = to_col(lf_row)
        li_col = to_col(li_row)
        bcum_row = jnp.sum(jnp.where(r <= c, jnp.broadcast_to(lf_col, (L, L)), 0.0), axis=0, keepdims=True)
        b_last = jnp.sum(lf_row, axis=1, keepdims=True)

        d_log = jnp.where(c <= r, bcum_col - bcum_row + li_row, -jnp.inf)
        inter = bcum_col + m_st
        m_t = jnp.maximum(inter, jnp.max(d_log, axis=1, keepdims=True))
        a_t = jnp.exp(inter - m_t)
        s_w = _dot_nt(q, k) * jnp.exp(d_log - m_t)
        num = a_t * _dot_nt(q, c_st.astype(BF16)) + _dot(s_w.astype(BF16), v)
        den = a_t * jnp.sum(q.astype(F32) * n_st, axis=1, keepdims=True) + jnp.sum(s_w, axis=1, keepdims=True)
        o_ref[0, pl.ds(off, L), :] = (num / jnp.maximum(jnp.abs(den), jnp.exp(-m_t))).astype(o_ref.dtype)

        w_end = b_last - bcum_col + li_col
        m_loc = jnp.max(w_end, axis=0, keepdims=True)
        e_end = jnp.exp(w_end - m_loc)
        c_loc = _dot_tn((e_end * v.astype(F32)).astype(BF16), k)
        n_loc = jnp.sum(e_end * k.astype(F32), axis=0, keepdims=True)
        m_new = jnp.maximum(b_last + m_st, m_loc)
        a = jnp.exp(b_last + m_st - m_new)
        gg = jnp.exp(m_loc - m_new)
        return a * c_st + gg * c_loc, a * n_st + gg * n_loc, m_new

    init = (jnp.zeros((ML_HEAD_DIM, ML_HEAD_DIM), F32), jnp.zeros((1, ML_HEAD_DIM), F32), jnp.zeros((1, 1), F32))
    lax.fori_loop(0, nc, chunk, init)


def _mlstm(qk3, pa3, grow, gbias):
    b, s, _ = pa3.shape
    vbase = (3 * SB_WIDTH + 2 * ML_WIDTH) // ML_HEAD_DIM
    return pl.pallas_call(
        _mlstm_kernel,
        grid=(b, ML_HEADS),
        in_specs=[
            pl.BlockSpec((1, s, ML_HEAD_DIM), lambda bi, h: (bi, 0, h)),
            pl.BlockSpec((1, s, ML_HEAD_DIM), lambda bi, h: (bi, 0, ML_HEADS + h)),
            pl.BlockSpec((1, s, ML_HEAD_DIM), lambda bi, h: (bi, 0, vbase + h)),
            pl.BlockSpec((1, 2 * ML_HEADS, s), lambda bi, h: (bi, 0, 0)),
            pl.BlockSpec((2 * ML_HEADS, 1), lambda bi, h: (0, 0)),
        ],
        out_specs=pl.BlockSpec((1, s, ML_HEAD_DIM), lambda bi, h: (bi, 0, h)),
        out_shape=jax.ShapeDtypeStruct((b, s, ML_WIDTH), BF16),
        compiler_params=_cparams("parallel", "parallel"),
        name="mlstm",
    )(qk3, qk3, pa3, grow, gbias)


def _merge_kernel(x_ref, osb_ref, hml_ref, mlo_ref, gate_ref, g0_ref, b0_ref, wsb_ref, wml_ref, wout_ref,
                  g1_ref, b1_ref, o_ref):
    h0 = _layer_norm(x_ref[...], g0_ref[...], b0_ref[...])
    o_ml = (_sigmoid(mlo_ref[...].astype(F32)) * hml_ref[...].astype(F32)).astype(BF16)
    y = _sigmoid(gate_ref[:, :D_MODEL].astype(F32)) * _dot(osb_ref[...], wsb_ref[...])
    y = y + _sigmoid(gate_ref[:, D_MODEL:].astype(F32)) * _dot(o_ml, wml_ref[...])
    mix = _dot(y.astype(BF16), wout_ref[...])
    o_ref[...] = _layer_norm(ALPHA * h0 + mix, g1_ref[...], b1_ref[...])


def _merge(x3, bi, osb, hml, pa, gate, g0, b0, wsb, wml, wout, g1, b1, tm=512):
    _, n, d = x3.shape
    const = lambda i: (0, 0)
    rowblk = lambda w: pl.BlockSpec((tm, w), lambda i: (i, 0))
    vec = pl.BlockSpec((1, d), const)
    mlo_blk = (3 * SB_WIDTH + 3 * ML_WIDTH) // ML_WIDTH
    return pl.pallas_call(
        _merge_kernel,
        grid=(n // tm,),
        in_specs=[
            pl.BlockSpec((None, tm, d), lambda i: (bi, i, 0)), rowblk(SB_WIDTH), rowblk(ML_WIDTH),
            pl.BlockSpec((tm, ML_WIDTH), lambda i: (i, mlo_blk)),
            rowblk(2 * d), vec, vec,
            pl.BlockSpec(wsb.shape, const), pl.BlockSpec(wml.shape, const), pl.BlockSpec(wout.shape, const),
            vec, vec,
        ],
        out_specs=rowblk(d),
        out_shape=jax.ShapeDtypeStruct((n, d), F32),
        compiler_params=_cparams("parallel"),
        name="merge_outproj_ln",
    )(x3, osb, hml, pa, gate, g0, b0, wsb, wml, wout, g1, b1)


def _topk_rows(s, k, ids=None):
    if ids is None:
        ids = lax.broadcasted_iota(I32, s.shape, 0)
    none = jnp.iinfo(jnp.int32).max
    vals, idxs = [], []
    for _ in range(k):
        m = jnp.max(s, axis=0, keepdims=True)
        am = jnp.min(jnp.where(s == m, ids, none), axis=0, keepdims=True)
        vals.append(m)
        idxs.append(am)
        s = jnp.where(ids == am, -jnp.inf, s)
    return jnp.concatenate(vals, axis=0), jnp.concatenate(idxs, axis=0)


SUBLANES = 8


def _pair_candidates(v1, v2):
    k = PEER_TOPK
    sub = lax.broadcasted_iota(I32, (SUBLANES, v1.shape[1]), 0)
    vals, ids = [], []
    i = 0
    while k // (i + 1) > 1:
        nj = k // (i + 1)
        for j0 in range(0, nj, SUBLANES):
            val = v1[i:i + 1, :] + v2[j0:j0 + SUBLANES, :]
            if nj - j0 < SUBLANES:
                val = jnp.where(sub < nj - j0, val, -jnp.inf)
            vals.append(val)
            ids.append(i * k + j0 + sub)
        i += 1
    assert (k - i) % SUBLANES == 0
    for i0 in range(i, k, SUBLANES):
        vals.append(v1[i0:i0 + SUBLANES, :] + v2[0:1, :])
        ids.append((i0 + sub) * k)
    return jnp.concatenate(vals, axis=0), jnp.concatenate(ids, axis=0)


def _select_rows(sel, table):
    out = jnp.zeros(sel.shape, table.dtype)
    for r_ in range(table.shape[0]):
        out = jnp.where(sel == r_, table[r_:r_ + 1, :], out)
    return out


def _peer_route_kernel(h_ref, wq_ref, k1_ref, k2_ref, idx_ref, gate_ref, q_scr, *, tt):
    q_scr[...] = _dot(h_ref[...].astype(BF16), wq_ref[...]).astype(BF16)
    nsub = h_ref.shape[0] // tt

    def one(it, _):
        head = it % PEER_HEADS
        sub = it // PEER_HEADS
        roff = pl.multiple_of(sub * tt, tt)
        coff = pl.multiple_of(head * PEER_QDIM, PEER_QDIM)
        q1 = q_scr[pl.ds(roff, tt), pl.ds(coff, PEER_HALF)]
        q2 = q_scr[pl.ds(roff, tt), pl.ds(coff + PEER_HALF, PEER_HALF)]
        v1, i1 = _topk_rows(_dot_nt(k1_ref[...], q1), PEER_TOPK)
        v2, i2 = _topk_rows(_dot_nt(k2_ref[...], q2), PEER_TOPK)
        cand, cand_pos = _pair_candidates(v1, v2)
        top_s, pos = _topk_rows(cand, PEER_TOPK, cand_pos)
        e1 = _select_rows(pos // PEER_TOPK, i1)
        e2 = _select_rows(pos % PEER_TOPK, i2)
        ex = jnp.exp(top_s - top_s[0:1, :])
        gates = ex / jnp.sum(ex, axis=0, keepdims=True)
        hoff = pl.multiple_of(head * PEER_TOPK, PEER_TOPK)
        idx_ref[pl.ds(hoff, PEER_TOPK), pl.ds(roff, tt)] = e1 * PEER_KEYS + e2
        gate_ref[pl.ds(hoff, PEER_TOPK), pl.ds(roff, tt)] = gates
        return 0

    lax.fori_loop(0, nsub * PEER_HEADS, one, 0)


def _peer_route(h1, tok0, n, wq, k1, k2, tm=256, tt=128):
    d = h1.shape[1]
    off = tok0 // tm
    const = lambda i: (0, 0)
    nsel = PEER_HEADS * PEER_TOPK
    return pl.pallas_call(
        functools.partial(_peer_route_kernel, tt=tt),
        grid=(n // tm,),
        in_specs=[
            pl.BlockSpec((tm, d), lambda i: (off + i, 0)),
            pl.BlockSpec(wq.shape, const),
            pl.BlockSpec(k1.shape, const),
            pl.BlockSpec(k2.shape, const),
        ],
        out_specs=[pl.BlockSpec((nsel, tm), lambda i: (0, i)), pl.BlockSpec((nsel, tm), lambda i: (0, i))],
        out_shape=[jax.ShapeDtypeStruct((nsel, n), I32), jax.ShapeDtypeStruct((nsel, n), F32)],
        scratch_shapes=[pltpu.VMEM((tm, PEER_HEADS * PEER_QDIM), BF16)],
        compiler_params=_cparams("parallel"),
        name="peer_route",
    )(h1, wq, k1, k2)


def _dot_split3(x, sel):
    hi = x.astype(BF16)
    r1 = x - hi.astype(F32)
    mid = r1.astype(BF16)
    lo = (r1 - mid.astype(F32)).astype(BF16)
    return _dot(hi, sel) + _dot(mid, sel) + _dot(lo, sel)


def _gelu_gate_kernel(part_ref, gate_ref, sel_ref, selt_ref, o_ref):
    a = _dot_split3(part_ref[...], sel_ref[...])
    w = gate_ref[...] * (0.5 * a * (1.0 + lax.erf(a * (2.0 ** -0.5))))
    o_ref[...] = _dot_split3(w, selt_ref[...])


def _gelu_gate(part, gates, tm=512):
    n, wide = part.shape
    nsel = gates.shape[1]
    lanes = wide // nsel
    sel = (jnp.arange(wide)[:, None] // lanes == jnp.arange(nsel)[None, :]).astype(BF16)
    const = lambda i: (0, 0)
    return pl.pallas_call(
        _gelu_gate_kernel,
        grid=(n // tm,),
        in_specs=[pl.BlockSpec((tm, wide), lambda i: (i, 0)), pl.BlockSpec((tm, nsel), lambda i: (i, 0)),
                  pl.BlockSpec((wide, nsel), const), pl.BlockSpec((nsel, wide), const)],
        out_specs=pl.BlockSpec((tm, wide), lambda i: (i, 0)),
        out_shape=jax.ShapeDtypeStruct((n, wide), F32),
        compiler_params=_cparams("parallel"),
        name="gelu_gate",
    )(part, gates, sel, sel.T)


def _final_kernel(h_ref, ffn_ref, p_ref, wg_ref, wp_ref, g_ref, b_ref, o_ref):
    h = h_ref[...]
    ple = _sigmoid(_dot(h.astype(BF16), wg_ref[...])) * _dot(p_ref[...].astype(BF16), wp_ref[...])
    o_ref[...] = _layer_norm(ALPHA * h + ffn_ref[...] + ple, g_ref[...], b_ref[...])


def _final(h1, tok0, ffn, p4, bi, wg, wp, g, b, tm=512):
    n, d = ffn.shape
    off = tok0 // tm
    const = lambda i: (0, 0)
    rowblk = lambda w: pl.BlockSpec((tm, w), lambda i: (i, 0))
    vec = pl.BlockSpec((1, d), const)
    return pl.pallas_call(
        _final_kernel,
        grid=(n // tm,),
        in_specs=[pl.BlockSpec((tm, d), lambda i: (off + i, 0)), rowblk(d),
                  pl.BlockSpec((None, None, tm, p4.shape[3]), lambda i: (0, bi, off + i, 0)),
                  pl.BlockSpec(wg.shape, const), pl.BlockSpec(wp.shape, const), vec, vec],
        out_specs=rowblk(d),
        out_shape=jax.ShapeDtypeStruct((n, d), F32),
        compiler_params=_cparams("parallel"),
        name="ple_final_ln",
    )(h1, ffn, p4, wg, wp, g, b)


SC_CORES = 2
SC_SUBCORES = 16
SC_LANES = 16
SC_WORKERS = SC_CORES * SC_SUBCORES
PEER_NSEL = PEER_HEADS * PEER_TOPK
SC_ROWS = 16
SC_NBUF = 4
SC_GROUP = 8


def _sc_mesh():
    return plsc.VectorSubcoreMesh(core_axis_name="c", subcore_axis_name="s", num_cores=SC_CORES,
                                  num_subcores=SC_SUBCORES)


def _sc_grouped_pipeline(ngroups, stage_copies, out_copy, tab_hbm, idx_v, buf, sems, begin_group, compute):
    nchunk = PEER_NSEL // SC_ROWS
    nsteps = SC_GROUP * nchunk
    assert nsteps % SC_NBUF == 0 and ngroups % 2 == 0

    def gather(gs, step, slot):
        t = step // nchunk
        c = step % nchunk
        return pltpu.make_async_copy(tab_hbm.at[idx_v.at[gs, t, pl.ds(c * SC_ROWS, SC_ROWS)]], buf.at[slot],
                                     sems.at[slot])

    for cp in stage_copies(0, 0):
        cp.start()
    for cp in stage_copies(0, 0):
        cp.wait()
    for b in range(SC_NBUF - 1):
        gather(0, b, b).start()
    for cp in stage_copies(1, 1):
        cp.start()

    def group_pair(gp, _):
        for gs in range(2):
            g = 2 * gp + gs

            @pl.when(g >= 2)
            def _():
                out_copy(g - 2, gs).wait()

            begin_group(gs)

            def ring_round(q, _):
                for b in range(SC_NBUF):
                    s = q * SC_NBUF + b
                    ahead = s + SC_NBUF - 1
                    slot_a = (b + SC_NBUF - 1) % SC_NBUF

                    @pl.when(ahead < nsteps)
                    def _():
                        gather(gs, ahead, slot_a).start()

                    if b > 0:
                        @pl.when(jnp.logical_and(ahead >= nsteps, g + 1 < ngroups))
                        def _():
                            if b == 1:
                                for cp in stage_copies(g + 1, 1 - gs):
                                    cp.wait()
                            gather(1 - gs, ahead - nsteps, slot_a).start()

                    gather(gs, s, b).wait()
                    compute(gs, s // nchunk, s % nchunk, b)
                return 0

            lax.fori_loop(0, nsteps // SC_NBUF, ring_round, 0)
            out_copy(g, gs).start()

            @pl.when(g + 2 < ngroups)
            def _():
                for cp in stage_copies(g + 2, gs):
                    cp.start()
        return 0

    lax.fori_loop(0, ngroups // 2, group_pair, 0)
    out_copy(ngroups - 2, 0).wait()
    out_copy(ngroups - 1, 1).wait()


def _peer_act_sc(h1, tok0, idx, u_tab):
    n = idx.shape[0]
    d = h1.shape[1]
    per_w = n // SC_WORKERS
    hold = 32

    @functools.partial(
        pl.kernel, mesh=_sc_mesh(),
        out_type=jax.ShapeDtypeStruct((n, PEER_NSEL * SC_LANES), F32),
        scratch_types=[
            pltpu.VMEM((2, SC_GROUP, d), F32),
            pltpu.VMEM((2, SC_GROUP, PEER_NSEL), I32),
            pltpu.VMEM((2, SC_GROUP, PEER_NSEL * SC_LANES), F32),
            pltpu.VMEM((SC_NBUF, SC_ROWS, d), F32),
            pltpu.SemaphoreType.DMA((SC_NBUF,)),
            pltpu.SemaphoreType.DMA((2,)),
            pltpu.SemaphoreType.DMA((2,)),
            pltpu.SemaphoreType.DMA((2,)),
        ],
        name="peer_act_sc",
    )
    def k(h_hbm, idx_hbm, u_hbm, out_hbm, h_v, idx_v, out_v, buf, sems, sem_h, sem_i, sem_o):
        base = (lax.axis_index("s") * SC_CORES + lax.axis_index("c")) * per_w

        def stage_copies(g, gs):
            row0 = base + g * SC_GROUP
            return (pltpu.make_async_copy(h_hbm.at[pl.ds(tok0 + row0, SC_GROUP)], h_v.at[gs], sem_h.at[gs]),
                    pltpu.make_async_copy(idx_hbm.at[pl.ds(row0, SC_GROUP)], idx_v.at[gs], sem_i.at[gs]))

        def out_copy(g, gs):
            return pltpu.make_async_copy(out_v.at[gs], out_hbm.at[pl.ds(base + g * SC_GROUP, SC_GROUP)],
                                         sem_o.at[gs])

        def compute(gs, t, c, slot):
            for jb in range(d // (hold * SC_LANES)):
                col0 = jb * hold * SC_LANES
                hv = [h_v[gs, t, pl.ds(col0 + jj * SC_LANES, SC_LANES)] for jj in range(hold)]

                @plsc.parallel_loop(0, SC_ROWS)
                def _(r):
                    ps = [buf[slot, r, pl.ds(col0 + jj * SC_LANES, SC_LANES)] * hv[jj] for jj in range(hold)]
                    while len(ps) > 1:
                        ps = [ps[i] + ps[i + 1] for i in range(0, len(ps), 2)]
                    dst = out_v.at[gs, t, pl.ds((c * SC_ROWS + r) * SC_LANES, SC_LANES)]
                    if jb == 0:
                        dst[...] = ps[0]
                    else:
                        plsc.addupdate(dst, ps[0])

        _sc_grouped_pipeline(per_w // SC_GROUP, stage_copies, out_copy, u_hbm, idx_v, buf, sems, lambda gs: None,
                             compute)

    return k(h1, idx, u_tab)


def _peer_combine_sc(wexp, idx, v_tab):
    n = idx.shape[0]
    d = v_tab.shape[1]
    per_w = n // SC_WORKERS
    nvec = d // SC_LANES
    ncol = 4

    @functools.partial(
        pl.kernel, mesh=_sc_mesh(),
        out_type=jax.ShapeDtypeStruct((n, d), F32),
        scratch_types=[
            pltpu.VMEM((2, SC_GROUP, PEER_NSEL * SC_LANES), F32),
            pltpu.VMEM((2, SC_GROUP, PEER_NSEL), I32),
            pltpu.VMEM((2, SC_GROUP, d), F32),
            pltpu.VMEM((SC_NBUF, SC_ROWS, d), F32),
            pltpu.SemaphoreType.DMA((SC_NBUF,)),
            pltpu.SemaphoreType.DMA((2,)),
            pltpu.SemaphoreType.DMA((2,)),
            pltpu.SemaphoreType.DMA((2,)),
        ],
        name="peer_combine_sc",
    )
    def k(w_hbm, idx_hbm, v_hbm, out_hbm, w_v, idx_v, o_v, buf, sems, sem_w, sem_i, sem_o):
        base = (lax.axis_index("s") * SC_CORES + lax.axis_index("c")) * per_w

        def stage_copies(g, gs):
            row0 = base + g * SC_GROUP
            return (pltpu.make_async_copy(w_hbm.at[pl.ds(row0, SC_GROUP)], w_v.at[gs], sem_w.at[gs]),
                    pltpu.make_async_copy(idx_hbm.at[pl.ds(row0, SC_GROUP)], idx_v.at[gs], sem_i.at[gs]))

        def out_copy(g, gs):
            return pltpu.make_async_copy(o_v.at[gs], out_hbm.at[pl.ds(base + g * SC_GROUP, SC_GROUP)], sem_o.at[gs])

        def begin_group(gs):
            def zero(i, _):
                o_v[gs, i // nvec, pl.ds((i % nvec) * SC_LANES, SC_LANES)] = jnp.zeros((SC_LANES,), F32)
                return 0

            lax.fori_loop(0, SC_GROUP * nvec, zero, 0)

        def compute(gs, t, c, slot):
            ws = [w_v[gs, t, pl.ds((c * SC_ROWS + r) * SC_LANES, SC_LANES)] for r in range(SC_ROWS)]

            @plsc.parallel_loop(0, nvec // ncol)
            def _(cb):
                col0 = cb * ncol * SC_LANES
                acc = [None] * ncol
                for r in range(SC_ROWS):
                    for kk in range(ncol):
                        x = ws[r] * buf[slot, r, pl.ds(col0 + kk * SC_LANES, SC_LANES)]
                        acc[kk] = x if acc[kk] is None else acc[kk] + x
                for kk in range(ncol):
                    plsc.addupdate(o_v.at[gs, t, pl.ds(col0 + kk * SC_LANES, SC_LANES)], acc[kk])

        _sc_grouped_pipeline(per_w // SC_GROUP, stage_copies, out_copy, v_hbm, idx_v, buf, sems, begin_group, compute)

    return k(wexp, idx, v_tab)


def _token_mixer_and_norm(x3, bi, g0, b0, wa, wif, wg, gbias, conv_w, conv_b, wsb, wml, wout, g1, b1):
    s = x3.shape[1]
    n_if = 2 * ML_HEADS
    pa, gif, gate = _inproj(x3, bi, g0, b0, wa, wif, wg)
    pa3 = pa.reshape(1, s, PA_WIDTH)
    o_sb = _sb_attention(pa3)
    qk3 = _conv_silu(pa3, conv_w, conv_b)
    grow = jnp.swapaxes(gif.reshape(1, s, LANES)[:, :, :n_if], 1, 2)
    h_ml = _mlstm(qk3, pa3, grow, gbias)
    return _merge(x3, bi, o_sb.reshape(s, SB_WIDTH), h_ml.reshape(s, ML_WIDTH), pa, gate, g0, b0, wsb, wml, wout,
                  g1, b1)


def _after(value, prev):
    if prev is None:
        return value
    return lax.optimization_barrier((value, prev))[0]


def kernel(x, p, ln0_g, ln0_b, w_in, b_igate, b_fgate, conv_w, conv_b, w_branch_sb, w_branch_ml, w_out, ln1_g, ln1_b, peer_wq, peer_k1, peer_k2, peer_u, peer_v, w_ple_gate, w_ple, ln2_g, ln2_b):
    b, s, d = x.shape
    assert w_in.shape[0] == DEPTH
    i = 0
    row = lambda v: v.reshape(1, -1)
    n_if = 2 * ML_HEADS
    wa = w_in[i][:, :PA_WIDTH].astype(BF16)
    wif = jnp.pad(w_in[i][:, PA_WIDTH:PA_WIDTH + n_if], ((0, 0), (0, LANES - n_if))).astype(BF16)
    wg = w_in[i][:, PA_WIDTH + n_if:].astype(BF16)
    gbias = jnp.concatenate([b_igate[i], b_fgate[i]]).reshape(n_if, 1)
    wsb, wml, wout = w_branch_sb[i].astype(BF16), w_branch_ml[i].astype(BF16), w_out[i].astype(BF16)
    wq, k1, k2 = peer_wq[i].astype(BF16), peer_k1[i].astype(BF16), peer_k2[i].astype(BF16)
    wpg, wp = w_ple_gate[i].astype(BF16), w_ple[i].astype(BF16)

    sp = s // 2
    pieces = [(0, 0, sp // 2), (0, sp // 2, sp // 2), (0, sp, sp)]
    pieces += [(e, hf * sp, sp) for e in range(1, b) for hf in range(2)]
    h1s, routed, parts, ffns, outs = {}, {}, {}, {}, {}
    last = None

    def mix(e):
        h1s[e] = _token_mixer_and_norm(x, e, _after(row(ln0_g), last), row(ln0_b), wa, wif, wg, gbias, conv_w[i],
                                       row(conv_b[i]), wsb, wml, wout, row(ln1_g[i]), row(ln1_b[i]))
        return h1s[e]

    def route(pc):
        e, tok0, n = pieces[pc]
        idx_t, gate_t = _peer_route(_after(h1s[e], last), tok0, n, wq, k1, k2)
        idx = idx_t.T
        routed[pc] = (idx, gate_t.T)
        parts[pc] = _peer_act_sc(h1s[e], tok0, _after(idx, ffns.get(pc - 2)), peer_u[i])
        return idx_t

    def gelu(pc):
        e, tok0, _ = pieces[pc]
        idx, gates = routed[pc]
        wexp = _gelu_gate(parts[pc], _after(gates, last))
        ffns[pc] = _peer_combine_sc(wexp, idx, peer_v[i])
        outs[pc] = _final(h1s[e], tok0, ffns[pc], p, e, wpg, wp, row(ln2_g[i]), row(ln2_b[i]))
        return wexp

    order = []
    for pc, (e, tok0, _) in enumerate(pieces):
        if tok0 == 0:
            order.append((mix, e))
        if pc >= 2:
            order.append((gelu, pc - 2))
        order.append((route, pc))
    order += [(gelu, len(pieces) - 2), (gelu, len(pieces) - 1)]
    for stage, arg in order:
        last = stage(arg)
    return jnp.concatenate([outs[pc] for pc in range(len(pieces))]).reshape(b, s, d)
```

```python
import functools

import jax
import jax.numpy as jnp
from jax import lax
from jax.experimental import pallas as pl
from jax.experimental.pallas import tpu as pltpu
from jax.experimental.pallas import tpu_sc as plsc

F32 = jnp.float32
BF16 = jnp.bfloat16
I32 = jnp.int32

D_MODEL = 1024
SB_HEADS = 8
SB_HEAD_DIM = 64
SB_WIDTH = SB_HEADS * SB_HEAD_DIM
ML_HEADS = 4
ML_HEAD_DIM = 128
ML_WIDTH = ML_HEADS * ML_HEAD_DIM
ML_CHUNK = 128
CONV_WIDTH = 4
PEER_HEADS = 8
PEER_KEYS = 128
PEER_QDIM = 256
PEER_HALF = PEER_QDIM // 2
PEER_TOPK = 16
DEPTH = 1
ALPHA = (2.0 * DEPTH) ** 0.25
LN_EPS = 1e-5

LANES = 128
VMEM_LIMIT = 56 * 1024 * 1024
SB_SKIP_LOG = -104.0

PA_WIDTH = 3 * SB_WIDTH + 4 * ML_WIDTH


def _cparams(*sem):
    return pltpu.CompilerParams(dimension_semantics=sem, vmem_limit_bytes=VMEM_LIMIT)


def _layer_norm(x, g, b):
    mu = jnp.mean(x, axis=-1, keepdims=True)
    xc = x - mu
    var = jnp.mean(xc * xc, axis=-1, keepdims=True)
    return xc * lax.rsqrt(var + LN_EPS) * g + b


def _log_sigmoid(z):
    return jnp.minimum(z, 0.0) - jnp.log1p(jnp.exp(-jnp.abs(z)))


def _sigmoid(z):
    return 1.0 / (1.0 + jnp.exp(-z))


def _dot(a, b):
    return jnp.dot(a, b, preferred_element_type=F32)


def _dot_nt(a, b):
    return lax.dot_general(a, b, (((1,), (1,)), ((), ())), preferred_element_type=F32)


def _dot_tn(a, b):
    return lax.dot_general(a, b, (((0,), (0,)), ((), ())), preferred_element_type=F32)


def _inproj_kernel(x_ref, g_ref, b_ref, wa_ref, wif_ref, wg_ref, oa_ref, oif_ref, og_ref, *, cw):
    h = _layer_norm(x_ref[...], g_ref[...], b_ref[...]).astype(BF16)
    for j in range(0, wa_ref.shape[1], cw):
        oa_ref[:, j:j + cw] = _dot(h, wa_ref[:, j:j + cw]).astype(BF16)
    oif_ref[...] = _dot(h, wif_ref[...])
    for j in range(0, wg_ref.shape[1], cw):
        og_ref[:, j:j + cw] = _dot(h, wg_ref[:, j:j + cw]).astype(BF16)


def _inproj(x3, bi, g, b, wa, wif, wg, tm=512):
    _, n, d = x3.shape
    const = lambda i: (0, 0)
    return pl.pallas_call(
        functools.partial(_inproj_kernel, cw=512),
        grid=(n // tm,),
        in_specs=[
            pl.BlockSpec((None, tm, d), lambda i: (bi, i, 0)),
            pl.BlockSpec((1, d), const),
            pl.BlockSpec((1, d), const),
            pl.BlockSpec(wa.shape, const),
            pl.BlockSpec(wif.shape, const),
            pl.BlockSpec(wg.shape, const),
        ],
        out_specs=[
            pl.BlockSpec((tm, wa.shape[1]), lambda i: (i, 0)),
            pl.BlockSpec((tm, wif.shape[1]), lambda i: (i, 0)),
            pl.BlockSpec((tm, wg.shape[1]), lambda i: (i, 0)),
        ],
        out_shape=[
            jax.ShapeDtypeStruct((n, wa.shape[1]), BF16),
            jax.ShapeDtypeStruct((n, wif.shape[1]), F32),
            jax.ShapeDtypeStruct((n, wg.shape[1]), BF16),
        ],
        compiler_params=_cparams("parallel"),
        name="ln_inproj",
    )(x3, g, b, wa, wif, wg)


def _sb_kernel(q_ref, k_ref, v_ref, o_ref, *, tq):
    qi = pl.program_id(2)
    q = q_ref[0]
    lane = lax.broadcasted_iota(I32, (1, LANES), 1)
    row = lax.broadcasted_iota(I32, (tq, tq), 0)
    col = lax.broadcasted_iota(I32, (tq, tq), 1)
    later = (row > col).astype(BF16)
    scale = SB_HEAD_DIM ** -0.5
    first = lane < SB_HEAD_DIM
    zero = jnp.zeros_like(q)
    q2 = jnp.concatenate([jnp.where(first, q, zero), jnp.where(first, zero, q)], axis=0)
    q2 = (q2.astype(F32) * scale).astype(BF16)
    row2 = lax.broadcasted_iota(I32, (2 * tq, tq), 0)
    causal = lax.broadcasted_iota(I32, (2 * tq, tq), 1) < jnp.where(row2 >= tq, row2 - tq, row2)

    def tile(qh, kb, carry, masked):
        off = pl.multiple_of(kb * tq, tq)
        k_blk = k_ref[0, pl.ds(off, tq), :]
        v_blk = v_ref[0, pl.ds(off, tq), :]
        z = _dot_nt(qh, k_blk)
        lb = _log_sigmoid(z)
        lom = lb - z
        if masked:
            lom = jnp.where(causal, lom, 0.0)
        hi = lom.astype(BF16)
        lo = (lom - hi.astype(F32)).astype(BF16)
        tail = _dot(hi, later) + _dot(lo, later) + carry
        w = jnp.exp(lb + tail)
        if masked:
            w = jnp.where(causal, w, 0.0)
        contrib = _dot(w.astype(BF16), v_blk)
        return contrib, carry + jnp.sum(lom, axis=1, keepdims=True)

    acc, carry = tile(q2, qi, jnp.zeros((2 * tq, 1), F32), True)

    def cond(st):
        kb, _, _, cmax = st
        return jnp.logical_and(kb >= 0, cmax > SB_SKIP_LOG)

    def body(st):
        kb, acc, carry, _ = st
        contrib, carry = tile(q2, kb, carry, False)
        return kb - 1, acc + contrib, carry, jnp.max(carry)

    _, acc, _, _ = lax.while_loop(cond, body, (qi - 1, acc, carry, jnp.max(carry)))
    o_ref[0] = jnp.where(first, acc[:tq], acc[tq:]).astype(o_ref.dtype)


def _sb_attention(pa3, tq=128):
    b, s, _ = pa3.shape
    npair = SB_WIDTH // LANES
    return pl.pallas_call(
        functools.partial(_sb_kernel, tq=tq),
        grid=(b, npair, s // tq),
        in_specs=[
            pl.BlockSpec((1, tq, LANES), lambda bi, hp, qi: (bi, qi, hp)),
            pl.BlockSpec((1, s, LANES), lambda bi, hp, qi: (bi, 0, npair + hp)),
            pl.BlockSpec((1, s, LANES), lambda bi, hp, qi: (bi, 0, 2 * npair + hp)),
        ],
        out_specs=pl.BlockSpec((1, tq, LANES), lambda bi, hp, qi: (bi, qi, hp)),
        out_shape=jax.ShapeDtypeStruct((b, s, SB_WIDTH), BF16),
        compiler_params=_cparams("parallel", "parallel", "arbitrary"),
        name="sb_attention",
    )(pa3, pa3, pa3)


CONV_HALO = 16


def _conv_kernel(x_ref, prev_ref, w_ref, b_ref, o_ref, buf_ref, *, ts):
    si = pl.program_id(1)
    is_k = pl.program_id(2)
    prev = prev_ref[0].astype(F32)
    buf_ref[0:CONV_HALO, :] = jnp.where(si == 0, 0.0, prev)
    buf_ref[CONV_HALO:, :] = x_ref[0].astype(F32)
    y = b_ref[...] + w_ref[0:1, :] * buf_ref[CONV_HALO:, :]
    for j in range(1, CONV_WIDTH):
        y = y + w_ref[j:j + 1, :] * buf_ref[CONV_HALO - j:CONV_HALO - j + ts, :]
    y = y * _sigmoid(y)
    y = y * jnp.where(is_k == 1, ML_HEAD_DIM ** -0.5, 1.0)
    o_ref[0] = y.astype(o_ref.dtype)


def _conv_silu(pa3, conv_w, conv_b, ts=1024):
    b, s, _ = pa3.shape
    ts = min(ts, s)
    cb = ML_WIDTH
    base = 3 * SB_WIDTH // cb
    hb = ts // CONV_HALO
    return pl.pallas_call(
        functools.partial(_conv_kernel, ts=ts),
        grid=(b, s // ts, 2),
        in_specs=[
            pl.BlockSpec((1, ts, cb), lambda bi, si, j: (bi, si, base + j)),
            pl.BlockSpec((1, CONV_HALO, cb), lambda bi, si, j: (bi, jnp.maximum(si * hb - 1, 0), base + j)),
            pl.BlockSpec((CONV_WIDTH, cb), lambda bi, si, j: (0, j)),
            pl.BlockSpec((1, cb), lambda bi, si, j: (0, j)),
        ],
        out_specs=pl.BlockSpec((1, ts, cb), lambda bi, si, j: (bi, si, j)),
        out_shape=jax.ShapeDtypeStruct((b, s, 2 * ML_WIDTH), BF16),
        scratch_shapes=[pltpu.VMEM((ts + CONV_HALO, cb), F32)],
        compiler_params=_cparams("parallel", "parallel", "parallel"),
        name="conv_silu",
    )(pa3, pa3, conv_w, conv_b)


def _mlstm_kernel(q_ref, k_ref, v_ref, g_ref, gb_ref, o_ref):
    L = ML_CHUNK
    head = pl.program_id(1)
    nc = q_ref.shape[1] // L
    r = lax.broadcasted_iota(I32, (L, L), 0)
    c = lax.broadcasted_iota(I32, (L, L), 1)
    eye = r == c
    sub = lax.broadcasted_iota(I32, (2 * ML_HEADS, L), 0)

    def to_col(row):
        return jnp.sum(jnp.where(eye, jnp.broadcast_to(row, (L, L)), 0.0), axis=1, keepdims=True)

    def chunk(ci, st):
        c_st, n_st, m_st = st
        off = pl.multiple_of(ci * L, L)
        q = q_ref[0, pl.ds(off, L), :]
        k = k_ref[0, pl.ds(off, L), :]
        v = v_ref[0, pl.ds(off, L), :]
        g = g_ref[0, :, pl.ds(off, L)] + gb_ref[...]
        li_row = jnp.sum(jnp.where(sub == head, g, 0.0), axis=0, keepdims=True)
        lf_row = _log_sigmoid(jnp.sum(jnp.where(sub == head + ML_HEADS, g, 0.0), axis=0, keepdims=True))
        lf_b = jnp.broadcast_to(lf_row, (L, L))
        bcum_col = jnp.sum(jnp.where(c <= r, lf_b, 0.0), axis=1, keepdims=True)
        lf_col = to_col(lf_row)
        li_col = to_col(li_row)
        bcum_row = jnp.sum(jnp.where(r <= c, jnp.broadcast_to(lf_col, (L, L)), 0.0), axis=0, keepdims=True)
        b_last = jnp.sum(lf_row, axis=1, keepdims=True)

        d_log = jnp.where(c <= r, bcum_col - bcum_row + li_row, -jnp.inf)
        inter = bcum_col + m_st
        m_t = jnp.maximum(inter, jnp.max(d_log, axis=1, keepdims=True))
        a_t = jnp.exp(inter - m_t)
        s_w = _dot_nt(q, k) * jnp.exp(d_log - m_t)
        num = a_t * _dot_nt(q, c_st.astype(BF16)) + _dot(s_w.astype(BF16), v)
        den = a_t * jnp.sum(q.astype(F32) * n_st, axis=1, keepdims=True) + jnp.sum(s_w, axis=1, keepdims=True)
        o_ref[0, pl.ds(off, L), :] = (num / jnp.maximum(jnp.abs(den), jnp.exp(-m_t))).astype(o_ref.dtype)

        w_end = b_last - bcum_col + li_col
        m_loc = jnp.max(w_end, axis=0, keepdims=True)
        e_end = jnp.exp(w_end - m_loc)
        c_loc = _dot_tn((e_end * v.astype(F32)).astype(BF16), k)
        n_loc = jnp.sum(e_end * k.astype(F32), axis=0, keepdims=True)
        m_new = jnp.maximum(b_last + m_st, m_loc)
        a = jnp.exp(b_last + m_st - m_new)
        gg = jnp.exp(m_loc - m_new)
        return a * c_st + gg * c_loc, a * n_st + gg * n_loc, m_new

    init = (jnp.zeros((ML_HEAD_DIM, ML_HEAD_DIM), F32), jnp.zeros((1, ML_HEAD_DIM), F32), jnp.zeros((1, 1), F32))
    lax.fori_loop(0, nc, chunk, init)


def _mlstm(qk3, pa3, grow, gbias):
    b, s, _ = pa3.shape
    vbase = (3 * SB_WIDTH + 2 * ML_WIDTH) // ML_HEAD_DIM
    return pl.pallas_call(
        _mlstm_kernel,
        grid=(b, ML_HEADS),
        in_specs=[
            pl.BlockSpec((1, s, ML_HEAD_DIM), lambda bi, h: (bi, 0, h)),
            pl.BlockSpec((1, s, ML_HEAD_DIM), lambda bi, h: (bi, 0, ML_HEADS + h)),
            pl.BlockSpec((1, s, ML_HEAD_DIM), lambda bi, h: (bi, 0, vbase + h)),
            pl.BlockSpec((1, 2 * ML_HEADS, s), lambda bi, h: (bi, 0, 0)),
            pl.BlockSpec((2 * ML_HEADS, 1), lambda bi, h: (0, 0)),
        ],
        out_specs=pl.BlockSpec((1, s, ML_HEAD_DIM), lambda bi, h: (bi, 0, h)),
        out_shape=jax.ShapeDtypeStruct((b, s, ML_WIDTH), BF16),
        compiler_params=_cparams("parallel", "parallel"),
        name="mlstm",
    )(qk3, qk3, pa3, grow, gbias)


def _merge_kernel(x_ref, osb_ref, hml_ref, mlo_ref, gate_ref, g0_ref, b0_ref, wsb_ref, wml_ref, wout_ref,
                  g1_ref, b1_ref, o_ref):
    h0 = _layer_norm(x_ref[...], g0_ref[...], b0_ref[...])
    o_ml = (_sigmoid(mlo_ref[...].astype(F32)) * hml_ref[...].astype(F32)).astype(BF16)
    y = _sigmoid(gate_ref[:, :D_MODEL].astype(F32)) * _dot(osb_ref[...], wsb_ref[...])
    y = y + _sigmoid(gate_ref[:, D_MODEL:].astype(F32)) * _dot(o_ml, wml_ref[...])
    mix = _dot(y.astype(BF16), wout_ref[...])
    o_ref[...] = _layer_norm(ALPHA * h0 + mix, g1_ref[...], b1_ref[...])


def _merge(x3, bi, osb, hml, pa, gate, g0, b0, wsb, wml, wout, g1, b1, tm=512):
    _, n, d = x3.shape
    const = lambda i: (0, 0)
    rowblk = lambda w: pl.BlockSpec((tm, w), lambda i: (i, 0))
    vec = pl.BlockSpec((1, d), const)
    mlo_blk = (3 * SB_WIDTH + 3 * ML_WIDTH) // ML_WIDTH
    return pl.pallas_call(
        _merge_kernel,
        grid=(n // tm,),
        in_specs=[
            pl.BlockSpec((None, tm, d), lambda i: (bi, i, 0)), rowblk(SB_WIDTH), rowblk(ML_WIDTH),
            pl.BlockSpec((tm, ML_WIDTH), lambda i: (i, mlo_blk)),
            rowblk(2 * d), vec, vec,
            pl.BlockSpec(wsb.shape, const), pl.BlockSpec(wml.shape, const), pl.BlockSpec(wout.shape, const),
            vec, vec,
        ],
        out_specs=rowblk(d),
        out_shape=jax.ShapeDtypeStruct((n, d), F32),
        compiler_params=_cparams("parallel"),
        name="merge_outproj_ln",
    )(x3, osb, hml, pa, gate, g0, b0, wsb, wml, wout, g1, b1)


def _topk_rows(s, k, ids=None):
    if ids is None:
        ids = lax.broadcasted_iota(I32, s.shape, 0)
    none = jnp.iinfo(jnp.int32).max
    vals, idxs = [], []
    for _ in range(k):
        m = jnp.max(s, axis=0, keepdims=True)
        am = jnp.min(jnp.where(s == m, ids, none), axis=0, keepdims=True)
        vals.append(m)
        idxs.append(am)
        s = jnp.where(ids == am, -jnp.inf, s)
    return jnp.concatenate(vals, axis=0), jnp.concatenate(idxs, axis=0)


SUBLANES = 8


def _pair_candidates(v1, v2):
    k = PEER_TOPK
    sub = lax.broadcasted_iota(I32, (SUBLANES, v1.shape[1]), 0)
    vals, ids = [], []
    i = 0
    while k // (i + 1) > 1:
        nj = k // (i + 1)
        for j0 in range(0, nj, SUBLANES):
            val = v1[i:i + 1, :] + v2[j0:j0 + SUBLANES, :]
            if nj - j0 < SUBLANES:
                val = jnp.where(sub < nj - j0, val, -jnp.inf)
            vals.append(val)
            ids.append(i * k + j0 + sub)
        i += 1
    assert (k - i) % SUBLANES == 0
    for i0 in range(i, k, SUBLANES):
        vals.append(v1[i0:i0 + SUBLANES, :] + v2[0:1, :])
        ids.append((i0 + sub) * k)
    return jnp.concatenate(vals, axis=0), jnp.concatenate(ids, axis=0)


def _select_rows(sel, table):
    out = jnp.zeros(sel.shape, table.dtype)
    for r_ in range(table.shape[0]):
        out = jnp.where(sel == r_, table[r_:r_ + 1, :], out)
    return out


def _peer_route_kernel(h_ref, wq_ref, k1_ref, k2_ref, idx_ref, gate_ref, q_scr, *, tt):
    q_scr[...] = _dot(h_ref[...].astype(BF16), wq_ref[...]).astype(BF16)
    nsub = h_ref.shape[0] // tt

    def one(it, _):
        head = it % PEER_HEADS
        sub = it // PEER_HEADS
        roff = pl.multiple_of(sub * tt, tt)
        coff = pl.multiple_of(head * PEER_QDIM, PEER_QDIM)
        q1 = q_scr[pl.ds(roff, tt), pl.ds(coff, PEER_HALF)]
        q2 = q_scr[pl.ds(roff, tt), pl.ds(coff + PEER_HALF, PEER_HALF)]
        v1, i1 = _topk_rows(_dot_nt(k1_ref[...], q1), PEER_TOPK)
        v2, i2 = _topk_rows(_dot_nt(k2_ref[...], q2), PEER_TOPK)
        cand, cand_pos = _pair_candidates(v1, v2)
        top_s, pos = _topk_rows(cand, PEER_TOPK, cand_pos)
        e1 = _select_rows(pos // PEER_TOPK, i1)
        e2 = _select_rows(pos % PEER_TOPK, i2)
        ex = jnp.exp(top_s - top_s[0:1, :])
        gates = ex / jnp.sum(ex, axis=0, keepdims=True)
        hoff = pl.multiple_of(head * PEER_TOPK, PEER_TOPK)
        idx_ref[pl.ds(hoff, PEER_TOPK), pl.ds(roff, tt)] = e1 * PEER_KEYS + e2
        gate_ref[pl.ds(hoff, PEER_TOPK), pl.ds(roff, tt)] = gates
        return 0

    lax.fori_loop(0, nsub * PEER_HEADS, one, 0)


def _peer_route(h1, tok0, n, wq, k1, k2, tm=256, tt=128):
    d = h1.shape[1]
    off = tok0 // tm
    const = lambda i: (0, 0)
    nsel = PEER_HEADS * PEER_TOPK
    return pl.pallas_call(
        functools.partial(_peer_route_kernel, tt=tt),
        grid=(n // tm,),
        in_specs=[
            pl.BlockSpec((tm, d), lambda i: (off + i, 0)),
            pl.BlockSpec(wq.shape, const),
            pl.BlockSpec(k1.shape, const),
            pl.BlockSpec(k2.shape, const),
        ],
        out_specs=[pl.BlockSpec((nsel, tm), lambda i: (0, i)), pl.BlockSpec((nsel, tm), lambda i: (0, i))],
        out_shape=[jax.ShapeDtypeStruct((nsel, n), I32), jax.ShapeDtypeStruct((nsel, n), F32)],
        scratch_shapes=[pltpu.VMEM((tm, PEER_HEADS * PEER_QDIM), BF16)],
        compiler_params=_cparams("parallel"),
        name="peer_route",
    )(h1, wq, k1, k2)


def _dot_split3(x, sel):
    hi = x.astype(BF16)
    r1 = x - hi.astype(F32)
    mid = r1.astype(BF16)
    lo = (r1 - mid.astype(F32)).astype(BF16)
    return _dot(hi, sel) + _dot(mid, sel) + _dot(lo, sel)


def _gelu_gate_kernel(part_ref, gate_ref, sel_ref, selt_ref, o_ref):
    a = _dot_split3(part_ref[...], sel_ref[...])
    w = gate_ref[...] * (0.5 * a * (1.0 + lax.erf(a * (2.0 ** -0.5))))
    o_ref[...] = _dot_split3(w, selt_ref[...])


def _gelu_gate(part, gates, tm=512):
    n, wide = part.shape
    nsel = gates.shape[1]
    lanes = wide // nsel
    sel = (jnp.arange(wide)[:, None] // lanes == jnp.arange(nsel)[None, :]).astype(BF16)
    const = lambda i: (0, 0)
    return pl.pallas_call(
        _gelu_gate_kernel,
        grid=(n // tm,),
        in_specs=[pl.BlockSpec((tm, wide), lambda i: (i, 0)), pl.BlockSpec((tm, nsel), lambda i: (i, 0)),
                  pl.BlockSpec((wide, nsel), const), pl.BlockSpec((nsel, wide), const)],
        out_specs=pl.BlockSpec((tm, wide), lambda i: (i, 0)),
        out_shape=jax.ShapeDtypeStruct((n, wide), F32),
        compiler_params=_cparams("parallel"),
        name="gelu_gate",
    )(part, gates, sel, sel.T)


def _final_kernel(h_ref, ffn_ref, p_ref, wg_ref, wp_ref, g_ref, b_ref, o_ref):
    h = h_ref[...]
    ple = _sigmoid(_dot(h.astype(BF16), wg_ref[...])) * _dot(p_ref[...].astype(BF16), wp_ref[...])
    o_ref[...] = _layer_norm(ALPHA * h + ffn_ref[...] + ple, g_ref[...], b_ref[...])


def _final(h1, tok0, ffn, p4, bi, wg, wp, g, b, tm=512):
    n, d = ffn.shape
    off = tok0 // tm
    const = lambda i: (0, 0)
    rowblk = lambda w: pl.BlockSpec((tm, w), lambda i: (i, 0))
    vec = pl.BlockSpec((1, d), const)
    return pl.pallas_call(
        _final_kernel,
        grid=(n // tm,),
        in_specs=[pl.BlockSpec((tm, d), lambda i: (off + i, 0)), rowblk(d),
                  pl.BlockSpec((None, None, tm, p4.shape[3]), lambda i: (0, bi, off + i, 0)),
                  pl.BlockSpec(wg.shape, const), pl.BlockSpec(wp.shape, const), vec, vec],
        out_specs=rowblk(d),
        out_shape=jax.ShapeDtypeStruct((n, d), F32),
        compiler_params=_cparams("parallel"),
        name="ple_final_ln",
    )(h1, ffn, p4, wg, wp, g, b)


SC_CORES = 2
SC_SUBCORES = 16
SC_LANES = 16
SC_WORKERS = SC_CORES * SC_SUBCORES
PEER_NSEL = PEER_HEADS * PEER_TOPK
SC_ROWS = 16
SC_NBUF = 4
SC_GROUP = 8


def _sc_mesh():
    return plsc.VectorSubcoreMesh(core_axis_name="c", subcore_axis_name="s", num_cores=SC_CORES,
                                  num_subcores=SC_SUBCORES)


def _sc_grouped_pipeline(ngroups, stage_copies, out_copy, tab_hbm, idx_v, buf, sems, begin_group, compute):
    nchunk = PEER_NSEL // SC_ROWS
    nsteps = SC_GROUP * nchunk
    assert nsteps % SC_NBUF == 0 and ngroups % 2 == 0

    def gather(gs, step, slot):
        t = step // nchunk
        c = step % nchunk
        return pltpu.make_async_copy(tab_hbm.at[idx_v.at[gs, t, pl.ds(c * SC_ROWS, SC_ROWS)]], buf.at[slot],
                                     sems.at[slot])

    for cp in stage_copies(0, 0):
        cp.start()
    for cp in stage_copies(0, 0):
        cp.wait()
    for b in range(SC_NBUF - 1):
        gather(0, b, b).start()
    for cp in stage_copies(1, 1):
        cp.start()

    def group_pair(gp, _):
        for gs in range(2):
            g = 2 * gp + gs

            @pl.when(g >= 2)
            def _():
                out_copy(g - 2, gs).wait()

            begin_group(gs)

            def ring_round(q, _):
                for b in range(SC_NBUF):
                    s = q * SC_NBUF + b
                    ahead = s + SC_NBUF - 1
                    slot_a = (b + SC_NBUF - 1) % SC_NBUF

                    @pl.when(ahead < nsteps)
                    def _():
                        gather(gs, ahead, slot_a).start()

                    if b > 0:
                        @pl.when(jnp.logical_and(ahead >= nsteps, g + 1 < ngroups))
                        def _():
                            if b == 1:
                                for cp in stage_copies(g + 1, 1 - gs):
                                    cp.wait()
                            gather(1 - gs, ahead - nsteps, slot_a).start()

                    gather(gs, s, b).wait()
                    compute(gs, s // nchunk, s % nchunk, b)
                return 0

            lax.fori_loop(0, nsteps // SC_NBUF, ring_round, 0)
            out_copy(g, gs).start()

            @pl.when(g + 2 < ngroups)
            def _():
                for cp in stage_copies(g + 2, gs):
                    cp.start()
        return 0

    lax.fori_loop(0, ngroups // 2, group_pair, 0)
    out_copy(ngroups - 2, 0).wait()
    out_copy(ngroups - 1, 1).wait()


def _peer_act_sc(h1, tok0, idx, u_tab):
    n = idx.shape[0]
    d = h1.shape[1]
    per_w = n // SC_WORKERS
    hold = 32

    @functools.partial(
        pl.kernel, mesh=_sc_mesh(),
        out_type=jax.ShapeDtypeStruct((n, PEER_NSEL * SC_LANES), F32),
        scratch_types=[
            pltpu.VMEM((2, SC_GROUP, d), F32),
            pltpu.VMEM((2, SC_GROUP, PEER_NSEL), I32),
            pltpu.VMEM((2, SC_GROUP, PEER_NSEL * SC_LANES), F32),
            pltpu.VMEM((SC_NBUF, SC_ROWS, d), F32),
            pltpu.SemaphoreType.DMA((SC_NBUF,)),
            pltpu.SemaphoreType.DMA((2,)),
            pltpu.SemaphoreType.DMA((2,)),
            pltpu.SemaphoreType.DMA((2,)),
        ],
        name="peer_act_sc",
    )
    def k(h_hbm, idx_hbm, u_hbm, out_hbm, h_v, idx_v, out_v, buf, sems, sem_h, sem_i, sem_o):
        base = (lax.axis_index("s") * SC_CORES + lax.axis_index("c")) * per_w

        def stage_copies(g, gs):
            row0 = base + g * SC_GROUP
            return (pltpu.make_async_copy(h_hbm.at[pl.ds(tok0 + row0, SC_GROUP)], h_v.at[gs], sem_h.at[gs]),
                    pltpu.make_async_copy(idx_hbm.at[pl.ds(row0, SC_GROUP)], idx_v.at[gs], sem_i.at[gs]))

        def out_copy(g, gs):
            return pltpu.make_async_copy(out_v.at[gs], out_hbm.at[pl.ds(base + g * SC_GROUP, SC_GROUP)],
                                         sem_o.at[gs])

        def compute(gs, t, c, slot):
            for jb in range(d // (hold * SC_LANES)):
                col0 = jb * hold * SC_LANES
                hv = [h_v[gs, t, pl.ds(col0 + jj * SC_LANES, SC_LANES)] for jj in range(hold)]

                @plsc.parallel_loop(0, SC_ROWS)
                def _(r):
                    ps = [buf[slot, r, pl.ds(col0 + jj * SC_LANES, SC_LANES)] * hv[jj] for jj in range(hold)]
                    while len(ps) > 1:
                        ps = [ps[i] + ps[i + 1] for i in range(0, len(ps), 2)]
                    dst = out_v.at[gs, t, pl.ds((c * SC_ROWS + r) * SC_LANES, SC_LANES)]
                    if jb == 0:
                        dst[...] = ps[0]
                    else:
                        plsc.addupdate(dst, ps[0])

        _sc_grouped_pipeline(per_w // SC_GROUP, stage_copies, out_copy, u_hbm, idx_v, buf, sems, lambda gs: None,
                             compute)

    return k(h1, idx, u_tab)


def _peer_combine_sc(wexp, idx, v_tab):
    n = idx.shape[0]
    d = v_tab.shape[1]
    per_w = n // SC_WORKERS
    nvec = d // SC_LANES
    ncol = 4

    @functools.partial(
        pl.kernel, mesh=_sc_mesh(),
        out_type=jax.ShapeDtypeStruct((n, d), F32),
        scratch_types=[
            pltpu.VMEM((2, SC_GROUP, PEER_NSEL * SC_LANES), F32),
            pltpu.VMEM((2, SC_GROUP, PEER_NSEL), I32),
            pltpu.VMEM((2, SC_GROUP, d), F32),
            pltpu.VMEM((SC_NBUF, SC_ROWS, d), F32),
            pltpu.SemaphoreType.DMA((SC_NBUF,)),
            pltpu.SemaphoreType.DMA((2,)),
            pltpu.SemaphoreType.DMA((2,)),
            pltpu.SemaphoreType.DMA((2,)),
        ],
        name="peer_combine_sc",
    )
    def k(w_hbm, idx_hbm, v_hbm, out_hbm, w_v, idx_v, o_v, buf, sems, sem_w, sem_i, sem_o):
        base = (lax.axis_index("s") * SC_CORES + lax.axis_index("c")) * per_w

        def stage_copies(g, gs):
            row0 = base + g * SC_GROUP
            return (pltpu.make_async_copy(w_hbm.at[pl.ds(row0, SC_GROUP)], w_v.at[gs], sem_w.at[gs]),
                    pltpu.make_async_copy(idx_hbm.at[pl.ds(row0, SC_GROUP)], idx_v.at[gs], sem_i.at[gs]))

        def out_copy(g, gs):
            return pltpu.make_async_copy(o_v.at[gs], out_hbm.at[pl.ds(base + g * SC_GROUP, SC_GROUP)], sem_o.at[gs])

        def begin_group(gs):
            def zero(i, _):
                o_v[gs, i // nvec, pl.ds((i % nvec) * SC_LANES, SC_LANES)] = jnp.zeros((SC_LANES,), F32)
                return 0

            lax.fori_loop(0, SC_GROUP * nvec, zero, 0)

        def compute(gs, t, c, slot):
            ws = [w_v[gs, t, pl.ds((c * SC_ROWS + r) * SC_LANES, SC_LANES)] for r in range(SC_ROWS)]

            @plsc.parallel_loop(0, nvec // ncol)
            def _(cb):
                col0 = cb * ncol * SC_LANES
                acc = [None] * ncol
                for r in range(SC_ROWS):
                    for kk in range(ncol):
                        x = ws[r] * buf[slot, r, pl.ds(col0 + kk * SC_LANES, SC_LANES)]
                        acc[kk] = x if acc[kk] is None else acc[kk] + x
                for kk in range(ncol):
                    plsc.addupdate(o_v.at[gs, t, pl.ds(col0 + kk * SC_LANES, SC_LANES)], acc[kk])

        _sc_grouped_pipeline(per_w // SC_GROUP, stage_copies, out_copy, v_hbm, idx_v, buf, sems, begin_group, compute)

    return k(wexp, idx, v_tab)


def _token_mixer_and_norm(x3, bi, g0, b0, wa, wif, wg, gbias, conv_w, conv_b, wsb, wml, wout, g1, b1):
    s = x3.shape[1]
    n_if = 2 * ML_HEADS
    pa, gif, gate = _inproj(x3, bi, g0, b0, wa, wif, wg)
    pa3 = pa.reshape(1, s, PA_WIDTH)
    o_sb = _sb_attention(pa3)
    qk3 = _conv_silu(pa3, conv_w, conv_b)
    grow = jnp.swapaxes(gif.reshape(1, s, LANES)[:, :, :n_if], 1, 2)
    h_ml = _mlstm(qk3, pa3, grow, gbias)
    return _merge(x3, bi, o_sb.reshape(s, SB_WIDTH), h_ml.reshape(s, ML_WIDTH), pa, gate, g0, b0, wsb, wml, wout,
                  g1, b1)


def _after(value, prev):
    if prev is None:
        return value
    return lax.optimization_barrier((value, prev))[0]


def kernel(x, p, ln0_g, ln0_b, w_in, b_igate, b_fgate, conv_w, conv_b, w_branch_sb, w_branch_ml, w_out, ln1_g, ln1_b, peer_wq, peer_k1, peer_k2, peer_u, peer_v, w_ple_gate, w_ple, ln2_g, ln2_b):
    b, s, d = x.shape
    assert w_in.shape[0] == DEPTH
    i = 0
    row = lambda v: v.reshape(1, -1)
    n_if = 2 * ML_HEADS
    wa = w_in[i][:, :PA_WIDTH].astype(BF16)
    wif = jnp.pad(w_in[i][:, PA_WIDTH:PA_WIDTH + n_if], ((0, 0), (0, LANES - n_if))).astype(BF16)
    wg = w_in[i][:, PA_WIDTH + n_if:].astype(BF16)
    gbias = jnp.concatenate([b_igate[i], b_fgate[i]]).reshape(n_if, 1)
    wsb, wml, wout = w_branch_sb[i].astype(BF16), w_branch_ml[i].astype(BF16), w_out[i].astype(BF16)
    wq, k1, k2 = peer_wq[i].astype(BF16), peer_k1[i].astype(BF16), peer_k2[i].astype(BF16)
    wpg, wp = w_ple_gate[i].astype(BF16), w_ple[i].astype(BF16)

    sp = s // 2
    pieces = [(0, 0, sp // 4), (0, sp // 4, sp // 4), (0, sp // 2, sp // 2), (0, sp, sp)]
    pieces += [(e, hf * sp, sp) for e in range(1, b) for hf in range(2)]
    h1s, routed, parts, ffns, outs = {}, {}, {}, {}, {}
    last = None

    def mix(e):
        h1s[e] = _token_mixer_and_norm(x, e, _after(row(ln0_g), last), row(ln0_b), wa, wif, wg, gbias, conv_w[i],
                                       row(conv_b[i]), wsb, wml, wout, row(ln1_g[i]), row(ln1_b[i]))
        return h1s[e]

    def route(pc):
        e, tok0, n = pieces[pc]
        idx_t, gate_t = _peer_route(_after(h1s[e], last), tok0, n, wq, k1, k2)
        idx = idx_t.T
        routed[pc] = (idx, gate_t.T)
        parts[pc] = _peer_act_sc(h1s[e], tok0, _after(idx, ffns.get(pc - 2)), peer_u[i])
        return idx_t

    def gelu(pc):
        e, tok0, _ = pieces[pc]
        idx, gates = routed[pc]
        wexp = _gelu_gate(parts[pc], _after(gates, last))
        ffns[pc] = _peer_combine_sc(wexp, idx, peer_v[i])
        outs[pc] = _final(h1s[e], tok0, ffns[pc], p, e, wpg, wp, row(ln2_g[i]), row(ln2_b[i]))
        return wexp

    order = []
    for pc, (e, tok0, _) in enumerate(pieces):
        if tok0 == 0:
            order.append((mix, e))
        if pc >= 2:
            order.append((gelu, pc - 2))
        order.append((route, pc))
    order += [(gelu, len(pieces) - 2), (gelu, len(pieces) - 1)]
    for stage, arg in order:
        last = stage(arg)
    return jnp.concatenate([outs[pc] for pc in range(len(pieces))]).reshape(b, s, d)
```
